```python
import jax, jax.numpy as jnp
from jax import lax
import numpy as np

D_MODEL = 1024
BATCH = 8
SEQ = 2048
DEPTH = 4

N_A = DEPTH // 2
N_B = DEPTH - N_A
HEAD_DIM = 64
MEM_LEN = 256
N_MEM_HEADS = 4
MEM_WIDTH = N_MEM_HEADS * HEAD_DIM
MIX_WIDTH = D_MODEL - MEM_WIDTH
CONV_CH = MIX_WIDTH
CONV_WIDTH = 31
N_FOX_HEADS = MIX_WIDTH // HEAD_DIM
D_FF = 2816
FFN_CONV_WIDTH = 3
BLOCK_Q = 128
RMS_EPS = 1e-6
LN_EPS = 1e-5

kernel_name = "yoco_conformer_fox_hybrid"


def rmsnorm(x, g):
    xf = x.astype(jnp.float32)
    y = xf * lax.rsqrt(jnp.mean(xf * xf, axis=-1, keepdims=True) + RMS_EPS)
    return (y * g.astype(jnp.float32)).astype(x.dtype)


def layernorm(x, g, b):
    xf = x.astype(jnp.float32)
    mu = jnp.mean(xf, axis=-1, keepdims=True)
    var = jnp.mean(jnp.square(xf - mu), axis=-1, keepdims=True)
    y = (xf - mu) * lax.rsqrt(var + LN_EPS)
    return (y * g.astype(jnp.float32) + b.astype(jnp.float32)).astype(x.dtype)


def causal_dwconv(x, w, b):
    width, ch = w.shape
    y = lax.conv_general_dilated(
        x, w[:, None, :].astype(x.dtype), window_strides=(1,),
        padding=((width - 1, 0),), dimension_numbers=("NWC", "WIO", "NWC"),
        feature_group_count=ch)
    return y + b


def conformer_conv(u, b_glu, w_dw, b_dw, ln_g, ln_b):
    u = u + b_glu
    a, gate = jnp.split(u, 2, axis=-1)
    v = a * jax.nn.sigmoid(gate)
    v = causal_dwconv(v, w_dw, b_dw)
    v = layernorm(v, ln_g, ln_b)
    return jax.nn.silu(v)


def memory_attention(q, mem_k, mem_v):
    b, s, _ = q.shape
    qh = q.reshape(b, s, N_MEM_HEADS, HEAD_DIM)
    kh = mem_k.reshape(b, -1, N_MEM_HEADS, HEAD_DIM)
    vh = mem_v.reshape(b, -1, N_MEM_HEADS, HEAD_DIM)
    logits = jnp.einsum("bshd,bmhd->bhsm", qh, kh).astype(jnp.float32) * (HEAD_DIM ** -0.5)
    p = jax.nn.softmax(logits, axis=-1).astype(vh.dtype)
    o = jnp.einsum("bhsm,bmhd->bshd", p, vh)
    return o.reshape(b, s, MEM_WIDTH)


def forgetting_attention(q, k, v, cum_logf):
    b, s, h, dh = q.shape
    scale = dh ** -0.5
    outs = []
    for i in range(s // BLOCK_Q):
        q0 = i * BLOCK_Q
        kend = q0 + BLOCK_Q
        qb = q[:, q0:kend]
        kb = k[:, :kend]
        vb = v[:, :kend]
        logits = jnp.einsum("bqhd,bkhd->bhqk", qb, kb).astype(jnp.float32) * scale
        logits = logits + cum_logf[:, :, q0:kend, None] - cum_logf[:, :, None, :kend]
        qpos = q0 + jnp.arange(BLOCK_Q)
        kpos = jnp.arange(kend)
        logits = jnp.where(kpos[None, :] <= qpos[:, None], logits, -jnp.inf)
        p = jax.nn.softmax(logits, axis=-1).astype(vb.dtype)
        outs.append(jnp.einsum("bhqk,bkhd->bqhd", p, vb))
    return jnp.concatenate(outs, axis=1)


def conv_ffn(h, w_up, w_dw, b_dw, w_down):
    u = h @ w_up
    u = causal_dwconv(u, w_dw, b_dw)
    gate, val = jnp.split(u, 2, axis=-1)
    return (jax.nn.silu(gate) * val) @ w_down


def setup_inputs(seed: int = 0) -> dict:
    key = jax.random.key(seed)
    ks = jax.random.split(key, 24)
    f32 = jnp.float32
    nrm = lambda k, shape, scale: jax.random.normal(k, shape, f32) * scale
    gain = lambda k, shape: 1.0 + 0.05 * jax.random.normal(k, shape, f32)
    d = D_MODEL
    return {
        "x": jax.random.normal(ks[0], (BATCH, SEQ, d), f32),
        "mem": jax.random.normal(ks[1], (BATCH, MEM_LEN, d), f32),
        "g_mix": gain(ks[2], (DEPTH, d)),
        "w_in_a": nrm(ks[3], (N_A, d, 2 * CONV_CH + MEM_WIDTH), d ** -0.5),
        "b_glu": nrm(ks[4], (N_A, 2 * CONV_CH), 0.02),
        "w_dw_a": nrm(ks[5], (N_A, CONV_WIDTH, CONV_CH), CONV_WIDTH ** -0.5),
        "b_dw_a": nrm(ks[6], (N_A, CONV_CH), 0.02),
        "ln_g": gain(ks[7], (N_A, CONV_CH)),
        "ln_b": nrm(ks[8], (N_A, CONV_CH), 0.02),
        "g_kv": gain(ks[9], (d,)),
        "w_kvf": nrm(ks[10], (d, 2 * MIX_WIDTH + N_FOX_HEADS), d ** -0.5),
        "b_f": 2.0 + 0.5 * jax.random.normal(ks[11], (N_FOX_HEADS,), f32),
        "w_in_b": nrm(ks[12], (N_B, d, MIX_WIDTH + MEM_WIDTH), d ** -0.5),
        "g_mem": gain(ks[13], (d,)),
        "w_mem_kv": nrm(ks[14], (DEPTH, d, 2 * MEM_WIDTH), d ** -0.5),
        "w_out": nrm(ks[15], (DEPTH, d, d), d ** -0.5),
        "g_ffn": gain(ks[16], (DEPTH, d)),
        "w_up": nrm(ks[17], (DEPTH, d, 2 * D_FF), d ** -0.5),
        "w_dw_f": nrm(ks[18], (DEPTH, FFN_CONV_WIDTH, 2 * D_FF), FFN_CONV_WIDTH ** -0.5),
        "b_dw_f": nrm(ks[19], (DEPTH, 2 * D_FF), 0.02),
        "w_down": nrm(ks[20], (DEPTH, D_FF, d), D_FF ** -0.5),
        "g_final": gain(ks[21], (d,)),
    }


def reference(x, mem, g_mix, w_in_a, b_glu, w_dw_a, b_dw_a, ln_g, ln_b, g_kv, w_kvf, b_f,
              w_in_b, g_mem, w_mem_kv, w_out, g_ffn, w_up, w_dw_f, b_dw_f, w_down, g_final):
    bsz, seq, _ = x.shape
    mem_n = rmsnorm(mem, g_mem)
    k_sh = v_sh = cum_logf = None
    for l in range(DEPTH):
        h = rmsnorm(x, g_mix[l])
        mem_k, mem_v = jnp.split(mem_n @ w_mem_kv[l], 2, axis=-1)
        if l < N_A:
            p = h @ w_in_a[l]
            u, q_mem = p[..., :2 * CONV_CH], p[..., 2 * CONV_CH:]
            mix = conformer_conv(u, b_glu[l], w_dw_a[l], b_dw_a[l], ln_g[l], ln_b[l])
        else:
            if l == N_A:
                hk = rmsnorm(x, g_kv)
                kvf = hk @ w_kvf
                k_sh = kvf[..., :MIX_WIDTH].reshape(bsz, seq, N_FOX_HEADS, HEAD_DIM)
                v_sh = kvf[..., MIX_WIDTH:2 * MIX_WIDTH].reshape(bsz, seq, N_FOX_HEADS, HEAD_DIM)
                f_logit = (kvf[..., 2 * MIX_WIDTH:] + b_f).astype(jnp.float32)
                cum_logf = jnp.cumsum(jax.nn.log_sigmoid(f_logit), axis=1).transpose(0, 2, 1)
            p = h @ w_in_b[l - N_A]
            q, q_mem = p[..., :MIX_WIDTH], p[..., MIX_WIDTH:]
            q = q.reshape(bsz, seq, N_FOX_HEADS, HEAD_DIM)
            mix = forgetting_attention(q, k_sh, v_sh, cum_logf).reshape(bsz, seq, MIX_WIDTH)
        mem_o = memory_attention(q_mem, mem_k, mem_v)
        x = x + jnp.concatenate([mix, mem_o], axis=-1) @ w_out[l]
        x = x + conv_ffn(rmsnorm(x, g_ffn[l]), w_up[l], w_dw_f[l], b_dw_f[l], w_down[l])
    return rmsnorm(x, g_final)
```

```python
import functools

import jax
import jax.numpy as jnp
from jax import lax
from jax.experimental import pallas as pl
from jax.experimental.pallas import tpu as pltpu

F32 = jnp.float32
BF16 = jnp.bfloat16

HEAD_DIM = 64
N_MEM_HEADS = 4
MEM_WIDTH = N_MEM_HEADS * HEAD_DIM
CONV_WIDTH = 31
FFN_CONV_WIDTH = 3
RMS_EPS = 1e-6
LN_EPS = 1e-5
ATTN_SCALE = HEAD_DIM ** -0.5

LANES = 128
SUBLANES = 8
CONV_HALO = 32
CONV_ROWS = 128
FFN_CHUNK = 256
NEG_BIG = -1e30

VMEM_LIMIT = 56 * 1024 * 1024


def _params(n_axes=2):
    return pltpu.CompilerParams(
        dimension_semantics=("arbitrary",) * n_axes, vmem_limit_bytes=VMEM_LIMIT)


def _const_spec(shape):
    nd = len(shape)
    return pl.BlockSpec(shape, lambda *_: (0,) * nd, pipeline_mode=pl.Buffered(1))


def _rms(x, g):
    return x * lax.rsqrt(jnp.mean(x * x, axis=-1, keepdims=True) + RMS_EPS) * g


def _sigmoid(x):
    return 1.0 / (1.0 + jnp.exp(-x))


def _dot(a, b):
    return jnp.dot(a, b, preferred_element_type=F32)


def _dot_nt(a, b):
    return lax.dot_general(a, b, (((1,), (1,)), ((), ())), preferred_element_type=F32)


def _mem_attention(q_mem, mem_k, mem_v):
    lane = lax.broadcasted_iota(jnp.int32, (1, MEM_WIDTH), 1)
    qs = q_mem * ATTN_SCALE
    out = jnp.zeros_like(q_mem)
    for h in range(N_MEM_HEADS):
        mask = (lane >= h * HEAD_DIM) & (lane < (h + 1) * HEAD_DIM)
        qh = jnp.where(mask, qs, 0.0).astype(BF16)
        s = _dot_nt(qh, mem_k)
        p = jnp.exp(s - jnp.max(s, axis=-1, keepdims=True))
        l = jnp.sum(p, axis=-1, keepdims=True)
        o = _dot(p.astype(BF16), mem_v)
        out = jnp.where(mask, o / l, out)
    return out


def _memkv_kernel(mem_ref, g_ref, w_ref, o_ref):
    mem_n = _rms(mem_ref[...], g_ref[...]).astype(BF16)
    o_ref[0] = _dot(mem_n, w_ref[0]).astype(BF16)


def _memkv(mem2d, g_mem, w_mem_kv):
    depth, d, n = w_mem_kv.shape
    rows = mem2d.shape[0]
    return pl.pallas_call(
        _memkv_kernel,
        grid=(depth,),
        in_specs=[
            _const_spec((rows, d)),
            _const_spec((1, d)),
            pl.BlockSpec((1, d, n), lambda l: (l, 0, 0)),
        ],
        out_specs=pl.BlockSpec((1, rows, n), lambda l: (l, 0, 0)),
        out_shape=jax.ShapeDtypeStruct((depth, rows, n), BF16),
        compiler_params=_params(1),
        name="memkv",
    )(mem2d, g_mem, w_mem_kv)


def _amix_kernel(x_ref, g_ref, win_ref, bglu_ref, wdw_ref, bdw_ref, lng_ref, lnb_ref, mkv_ref,
                 wout_ref, o_ref, vfull, vsh, conv_s):
    si = pl.program_id(1)
    t = x_ref.shape[1]
    ch = vfull.shape[1]
    x = x_ref[0]
    h = _rms(x, g_ref[...]).astype(BF16)
    p = _dot(h, win_ref[...])
    u = p[:, :2 * ch] + bglu_ref[...]
    v = u[:, :ch] * _sigmoid(u[:, ch:])

    @pl.when(si == 0)
    def _():
        vfull[0:CONV_HALO, :] = jnp.zeros((CONV_HALO, ch), F32)

    vfull[CONV_HALO:CONV_HALO + t, :] = v
    for k in range(1, SUBLANES):
        vsh[k - 1, SUBLANES:CONV_HALO + t, :] = vfull[SUBLANES - k:CONV_HALO + t - k, :]

    def conv_rows(r, carry):
        base = r * CONV_ROWS
        for c in range(ch // LANES):
            cols = slice(c * LANES, (c + 1) * LANES)
            acc = jnp.zeros((CONV_ROWS // SUBLANES, SUBLANES, LANES), F32)
            for j in range(CONV_WIDTH):
                back = CONV_WIDTH - 1 - j
                k, m = back % SUBLANES, back // SUBLANES
                start = pl.multiple_of(base + (CONV_HALO - SUBLANES * m), SUBLANES)
                if k == 0:
                    blk = vfull[pl.ds(start, CONV_ROWS), cols]
                else:
                    blk = vsh[k - 1, pl.ds(start, CONV_ROWS), cols]
                acc = acc + blk.reshape(CONV_ROWS // SUBLANES, SUBLANES, LANES) * wdw_ref[j, :, cols][None]
            conv_s[pl.ds(pl.multiple_of(base, CONV_ROWS), CONV_ROWS), cols] = acc.reshape(CONV_ROWS, LANES)
        return carry

    lax.fori_loop(0, t // CONV_ROWS, conv_rows, 0)
    vfull[0:CONV_HALO, :] = vfull[t:t + CONV_HALO, :]

    cv = conv_s[...] + bdw_ref[...]
    mu = jnp.mean(cv, axis=-1, keepdims=True)
    cen = cv - mu
    var = jnp.mean(cen * cen, axis=-1, keepdims=True)
    y = cen * lax.rsqrt(var + LN_EPS) * lng_ref[...] + lnb_ref[...]
    mix = y * _sigmoid(y)

    mem_o = _mem_attention(p[:, 2 * ch:], mkv_ref[0, :, :MEM_WIDTH], mkv_ref[0, :, MEM_WIDTH:])
    cat = jnp.concatenate([mix.astype(BF16), mem_o.astype(BF16)], axis=-1)
    o_ref[0] = x + _dot(cat, wout_ref[...])


def _amix(x, layer, g, w_in, b_glu, w_dw8, b_dw, ln_g, ln_b, memkv, w_out, tile):
    b, s, d = x.shape
    ch = b_dw.shape[-1]
    mem_len = memkv.shape[1] // b
    tok = pl.BlockSpec((1, tile, d), lambda bi, si: (bi, si, 0))
    return pl.pallas_call(
        _amix_kernel,
        grid=(b, s // tile),
        in_specs=[
            tok,
            _const_spec(g.shape),
            _const_spec(w_in.shape),
            _const_spec(b_glu.shape),
            _const_spec(w_dw8.shape),
            _const_spec(b_dw.shape),
            _const_spec(ln_g.shape),
            _const_spec(ln_b.shape),
            pl.BlockSpec((1, mem_len, 2 * MEM_WIDTH), lambda bi, si: (layer, bi, 0)),
            _const_spec(w_out.shape),
        ],
        out_specs=tok,
        out_shape=jax.ShapeDtypeStruct(x.shape, F32),
        scratch_shapes=[
            pltpu.VMEM((CONV_HALO + tile, ch), F32),
            pltpu.VMEM((SUBLANES - 1, CONV_HALO + tile, ch), F32),
            pltpu.VMEM((tile, ch), F32),
        ],
        compiler_params=_params(),
        name=f"amix{layer}",
    )(x, g, w_in, b_glu, w_dw8, b_dw, ln_g, ln_b, memkv, w_out)


def _ffn_kernel(x_ref, g_ref, wup_ref, wdw_ref, bdw_ref, wdown_ref, gfin_ref, o_ref, act_s, ucarry,
                *, final_norm):
    si = pl.program_id(1)
    t = x_ref.shape[1]
    dff = wdown_ref.shape[0]
    x = x_ref[0]
    h = _rms(x, g_ref[...]).astype(BF16)

    @pl.when(si == 0)
    def _():
        ucarry[...] = jnp.zeros_like(ucarry)

    def conv3(cols):
        u = _dot(h, wup_ref[:, cols])
        ext = jnp.concatenate([ucarry[:, cols], u], axis=0)
        ucarry[:, cols] = u[t - SUBLANES:, :]
        w = wdw_ref[:, cols]
        return (u * w[2:3] + ext[SUBLANES - 1:SUBLANES - 1 + t] * w[1:2]
                + ext[SUBLANES - 2:SUBLANES - 2 + t] * w[0:1] + bdw_ref[:, cols])

    for c in range(dff // FFN_CHUNK):
        gate = conv3(slice(c * FFN_CHUNK, (c + 1) * FFN_CHUNK))
        val = conv3(slice(dff + c * FFN_CHUNK, dff + (c + 1) * FFN_CHUNK))
        act_s[:, c * FFN_CHUNK:(c + 1) * FFN_CHUNK] = (gate * _sigmoid(gate) * val).astype(BF16)

    y = x + _dot(act_s[...], wdown_ref[...])
    if final_norm:
        y = _rms(y, gfin_ref[...])
    o_ref[0] = y


def _ffn(x, g, w_up, w_dw, b_dw, w_down, g_final, final_norm, tile):
    b, s, d = x.shape
    dff = w_down.shape[0]
    assert dff % FFN_CHUNK == 0
    tok = pl.BlockSpec((1, tile, d), lambda bi, si: (bi, si, 0))
    return pl.pallas_call(
        functools.partial(_ffn_kernel, final_norm=final_norm),
        grid=(b, s // tile),
        in_specs=[
            tok,
            _const_spec(g.shape),
            _const_spec(w_up.shape),
            _const_spec(w_dw.shape),
            _const_spec(b_dw.shape),
            _const_spec(w_down.shape),
            _const_spec(g_final.shape),
        ],
        out_specs=tok,
        out_shape=jax.ShapeDtypeStruct(x.shape, F32),
        scratch_shapes=[
            pltpu.VMEM((tile, dff), BF16),
            pltpu.VMEM((SUBLANES, 2 * dff), F32),
        ],
        compiler_params=_params(),
        name="ffn_final" if final_norm else "ffn",
    )(x, g, w_up, w_dw, b_dw, w_down, g_final)


def _kvf_kernel(x_ref, g_ref, w_ref, bf_ref, k_ref, v_ref, ccol_ref, crow_ref, carry):
    si = pl.program_id(1)
    t = x_ref.shape[1]
    mixw = k_ref.shape[2]
    nh = ccol_ref.shape[2]

    @pl.when(si == 0)
    def _():
        carry[...] = jnp.zeros_like(carry)

    hk = _rms(x_ref[0], g_ref[...]).astype(BF16)
    kvf = _dot(hk, w_ref[...])
    k_ref[0] = kvf[:, :mixw].astype(BF16)
    v_ref[0] = kvf[:, mixw:2 * mixw].astype(BF16)
    f = kvf[:, 2 * mixw:] + bf_ref[...]
    log_f = jnp.minimum(f, 0.0) - jnp.log1p(jnp.exp(-jnp.abs(f)))
    tri = (lax.broadcasted_iota(jnp.int32, (t, t), 1) <= lax.broadcasted_iota(jnp.int32, (t, t), 0)).astype(F32)
    cum = jnp.dot(tri, log_f, precision=lax.Precision.HIGHEST, preferred_element_type=F32) + carry[0:1, :]
    carry[...] = jnp.broadcast_to(cum[t - 1:t, :], carry.shape)
    ccol_ref[0] = cum[:, :nh]
    crow_ref[0] = cum.T[:nh, :]


def _kvf(x, g, w_pad, bf_pad, mixw, nh_pad, tile):
    b, s, d = x.shape
    tok = pl.BlockSpec((1, tile, d), lambda bi, si: (bi, si, 0))
    kv_spec = pl.BlockSpec((1, tile, mixw), lambda bi, si: (bi, si, 0))
    return pl.pallas_call(
        _kvf_kernel,
        grid=(b, s // tile),
        in_specs=[tok, _const_spec(g.shape), _const_spec(w_pad.shape), _const_spec(bf_pad.shape)],
        out_specs=[
            kv_spec,
            kv_spec,
            pl.BlockSpec((1, tile, nh_pad), lambda bi, si: (bi, si, 0)),
            pl.BlockSpec((1, nh_pad, tile), lambda bi, si: (bi, 0, si)),
        ],
        out_shape=[
            jax.ShapeDtypeStruct((b, s, mixw), BF16),
            jax.ShapeDtypeStruct((b, s, mixw), BF16),
            jax.ShapeDtypeStruct((b, s, nh_pad), F32),
            jax.ShapeDtypeStruct((b, nh_pad, s), F32),
        ],
        scratch_shapes=[pltpu.VMEM((SUBLANES, LANES), F32)],
        compiler_params=_params(),
        name="kvf",
    )(x, g, w_pad, bf_pad)


def _qproj_kernel(x_ref, g_ref, w_ref, mkv_ref, q_ref, mo_ref):
    mixw = q_ref.shape[2]
    h = _rms(x_ref[0], g_ref[...]).astype(BF16)
    p = _dot(h, w_ref[...])
    q_ref[0] = (p[:, :mixw] * ATTN_SCALE).astype(BF16)
    mo_ref[0] = _mem_attention(p[:, mixw:], mkv_ref[0, :, :MEM_WIDTH], mkv_ref[0, :, MEM_WIDTH:]).astype(BF16)


def _qproj(x, layer, g, w_in, memkv, mixw, tile):
    b, s, d = x.shape
    mem_len = memkv.shape[1] // b
    tok = pl.BlockSpec((1, tile, d), lambda bi, si: (bi, si, 0))
    return pl.pallas_call(
        _qproj_kernel,
        grid=(b, s // tile),
        in_specs=[
            tok,
            _const_spec(g.shape),
            _const_spec(w_in.shape),
            pl.BlockSpec((1, mem_len, 2 * MEM_WIDTH), lambda bi, si: (layer, bi, 0)),
        ],
        out_specs=[
            pl.BlockSpec((1, tile, mixw), lambda bi, si: (bi, si, 0)),
            pl.BlockSpec((1, tile, MEM_WIDTH), lambda bi, si: (bi, si, 0)),
        ],
        out_shape=[
            jax.ShapeDtypeStruct((b, s, mixw), BF16),
            jax.ShapeDtypeStruct((b, s, MEM_WIDTH), BF16),
        ],
        compiler_params=_params(),
        name=f"qproj{layer}",
    )(x, g, w_in, memkv)


def _fox_kernel(q_ref, k_ref, v_ref, ccol_ref, crow_ref, o_ref):
    qi = pl.program_id(1)
    tq = q_ref.shape[1]
    n_pairs = q_ref.shape[2] // LANES
    lane = lax.broadcasted_iota(jnp.int32, (1, LANES), 1)
    low = lane < HEAD_DIM
    causal = (lax.broadcasted_iota(jnp.int32, (tq, tq), 1) <= lax.broadcasted_iota(jnp.int32, (tq, tq), 0))

    for j in range(n_pairs):
        cols = slice(j * LANES, (j + 1) * LANES)
        q_pair = q_ref[0, :, cols]
        outs = []
        for hh in range(2):
            head = 2 * j + hh
            qm = jnp.where(low if hh == 0 else jnp.logical_not(low), q_pair, jnp.zeros_like(q_pair))
            c_t = ccol_ref[0, :, head:head + 1]

            def step(i, carry, diagonal, qm=qm, c_t=c_t, head=head, cols=cols):
                m, l, acc = carry
                ks = pl.multiple_of(i * tq, tq)
                s = _dot_nt(qm, k_ref[0, pl.ds(ks, tq), cols])
                s = s + (c_t - crow_ref[0, head:head + 1, pl.ds(ks, tq)])
                if diagonal:
                    s = jnp.where(causal, s, -jnp.inf)
                m_new = jnp.maximum(m, jnp.max(s, axis=-1, keepdims=True))
                alpha = jnp.exp(m - m_new)
                p = jnp.exp(s - m_new)
                l = alpha * l + jnp.sum(p, axis=-1, keepdims=True)
                acc = alpha * acc + _dot(p.astype(BF16), v_ref[0, pl.ds(ks, tq), cols])
                return m_new, l, acc

            init = (jnp.full((tq, 1), NEG_BIG, F32), jnp.zeros((tq, 1), F32), jnp.zeros((tq, LANES), F32))
            carry = lax.fori_loop(0, qi, functools.partial(step, diagonal=False), init)
            _, l, acc = step(qi, carry, diagonal=True)
            outs.append(acc / l)
        o_ref[0, :, cols] = jnp.where(low, outs[0], outs[1]).astype(BF16)


def _fox(q, k, v, ccol, crow, tile):
    b, s, mixw = q.shape
    nh_pad = ccol.shape[2]
    qspec = pl.BlockSpec((1, tile, mixw), lambda bi, si: (bi, si, 0))
    seq = pl.BlockSpec((1, s, mixw), lambda bi, si: (bi, 0, 0))
    return pl.pallas_call(
        _fox_kernel,
        grid=(b, s // tile),
        in_specs=[
            qspec, seq, seq,
            pl.BlockSpec((1, tile, nh_pad), lambda bi, si: (bi, si, 0)),
            pl.BlockSpec((1, nh_pad, s), lambda bi, si: (bi, 0, 0)),
        ],
        out_specs=qspec,
        out_shape=jax.ShapeDtypeStruct(q.shape, BF16),
        compiler_params=_params(),
        name="fox",
    )(q, k, v, ccol, crow)


def _outproj_kernel(x_ref, mix_ref, mo_ref, w_ref, o_ref):
    cat = jnp.concatenate([mix_ref[0], mo_ref[0]], axis=-1)
    o_ref[0] = x_ref[0] + _dot(cat, w_ref[...])


def _outproj(x, mix, mo, w_out, tile):
    b, s, d = x.shape
    tok = pl.BlockSpec((1, tile, d), lambda bi, si: (bi, si, 0))
    return pl.pallas_call(
        _outproj_kernel,
        grid=(b, s // tile),
        in_specs=[
            tok,
            pl.BlockSpec((1, tile, mix.shape[2]), lambda bi, si: (bi, si, 0)),
            pl.BlockSpec((1, tile, mo.shape[2]), lambda bi, si: (bi, si, 0)),
            _const_spec(w_out.shape),
        ],
        out_specs=tok,
        out_shape=jax.ShapeDtypeStruct(x.shape, F32),
        compiler_params=_params(),
        name="outproj",
    )(x, mix, mo, w_out)


def kernel(x, mem, g_mix, w_in_a, b_glu, w_dw_a, b_dw_a, ln_g, ln_b, g_kv, w_kvf, b_f, w_in_b, g_mem,
           w_mem_kv, w_out, g_ffn, w_up, w_dw_f, b_dw_f, w_down, g_final):
    bsz, seq, d = x.shape
    depth = g_mix.shape[0]
    n_a = w_in_a.shape[0]
    mixw = w_dw_a.shape[2]
    n_heads = b_f.shape[0]
    nh_pad = 16
    assert n_heads <= nh_pad and mixw % LANES == 0

    row = lambda a: a.reshape(1, -1)
    memkv = _memkv(mem.reshape(-1, d), row(g_mem), w_mem_kv.astype(BF16))

    kvf_cols = 2 * mixw + LANES
    w_kvf_pad = jnp.pad(w_kvf, ((0, 0), (0, kvf_cols - w_kvf.shape[1]))).astype(BF16)
    bf_pad = jnp.pad(b_f, (0, LANES - n_heads)).reshape(1, LANES)

    k_sh = v_sh = ccol = crow = None
    for l in range(depth):
        if l < n_a:
            w_dw8 = jnp.broadcast_to(w_dw_a[l][:, None, :], (CONV_WIDTH, SUBLANES, mixw))
            x = _amix(x, l, row(g_mix[l]), w_in_a[l].astype(BF16), row(b_glu[l]), w_dw8, row(b_dw_a[l]),
                      row(ln_g[l]), row(ln_b[l]), memkv, w_out[l].astype(BF16), tile=256)
        else:
            if l == n_a:
                k_sh, v_sh, ccol, crow = _kvf(x, row(g_kv), w_kvf_pad, bf_pad, mixw, nh_pad, tile=512)
            q, mem_o = _qproj(x, l, row(g_mix[l]), w_in_b[l - n_a].astype(BF16), memkv, mixw, tile=512)
            mix = _fox(q, k_sh, v_sh, ccol, crow, tile=256)
            x = _outproj(x, mix, mem_o, w_out[l].astype(BF16), tile=512)
        x = _ffn(x, row(g_ffn[l]), w_up[l].astype(BF16), w_dw_f[l], row(b_dw_f[l]), w_down[l].astype(BF16),
                 row(g_final), final_norm=(l == depth - 1), tile=512)
    return x
```

```python
import functools
import math

import numpy as np
import jax
import jax.numpy as jnp
from jax import lax
from jax.experimental import pallas as pl
from jax.experimental.pallas import tpu as pltpu

F32 = jnp.float32
BF16 = jnp.bfloat16

HEAD_DIM = 64
N_MEM_HEADS = 4
MEM_WIDTH = N_MEM_HEADS * HEAD_DIM
CONV_WIDTH = 31
FFN_CONV_WIDTH = 3
RMS_EPS = 1e-6
LN_EPS = 1e-5
ATTN_SCALE = HEAD_DIM ** -0.5
LOG2E = math.log2(math.e)

LANES = 128
SUBLANES = 8
BF16_ROWS = 16
CONV_HALO = 32
CONV_ROWS = 128
FFN_CHUNK = 256
FOX_TILE = 256
GATE_PARTS = 3
NEG_BIG = -1e30

VMEM_LIMIT = 56 * 1024 * 1024


def _params(n_axes=2):
    return pltpu.CompilerParams(
        dimension_semantics=("arbitrary",) * n_axes, vmem_limit_bytes=VMEM_LIMIT)


def _const_spec(shape):
    nd = len(shape)
    return pl.BlockSpec(shape, lambda *_: (0,) * nd, pipeline_mode=pl.Buffered(1))


def _rms(x, g):
    return x * lax.rsqrt(jnp.mean(x * x, axis=-1, keepdims=True) + RMS_EPS) * g


def _sigmoid(x):
    return 1.0 / (1.0 + jnp.exp(-x))


def _dot(a, b):
    return jnp.dot(a, b, preferred_element_type=F32)


def _dot_nt(a, b):
    return lax.dot_general(a, b, (((1,), (1,)), ((), ())), preferred_element_type=F32)


def _mem_attention(q_mem, mem_k, mem_v):
    lane = lax.broadcasted_iota(jnp.int32, (1, MEM_WIDTH), 1)
    qs = q_mem * ATTN_SCALE
    out = jnp.zeros_like(q_mem)
    for h in range(N_MEM_HEADS):
        mask = (lane >= h * HEAD_DIM) & (lane < (h + 1) * HEAD_DIM)
        qh = jnp.where(mask, qs, 0.0).astype(BF16)
        s = _dot_nt(qh, mem_k)
        p = jnp.exp(s - jnp.max(s, axis=-1, keepdims=True))
        l = jnp.sum(p, axis=-1, keepdims=True)
        o = _dot(p.astype(BF16), mem_v)
        out = jnp.where(mask, o / l, out)
    return out


def _memkv_kernel(mem_ref, g_ref, w_ref, o_ref):
    mem_n = _rms(mem_ref[...], g_ref[...]).astype(BF16)
    o_ref[0] = _dot(mem_n, w_ref[0]).astype(BF16)


def _memkv(mem2d, g_mem, w_mem_kv):
    depth, d, n = w_mem_kv.shape
    rows = mem2d.shape[0]
    return pl.pallas_call(
        _memkv_kernel,
        grid=(depth,),
        in_specs=[
            _const_spec((rows, d)),
            _const_spec((1, d)),
            pl.BlockSpec((1, d, n), lambda l: (l, 0, 0)),
        ],
        out_specs=pl.BlockSpec((1, rows, n), lambda l: (l, 0, 0)),
        out_shape=jax.ShapeDtypeStruct((depth, rows, n), BF16),
        compiler_params=_params(1),
        name="memkv",
    )(mem2d, g_mem, w_mem_kv)


def _amix_kernel(x_ref, g_ref, win_ref, bglu_ref, wdw_ref, bdw_ref, lng_ref, lnb_ref, mkv_ref,
                 wout_ref, o_ref, vfull, vsh, conv_s):
    si = pl.program_id(1)
    t = x_ref.shape[1]
    ch = vfull.shape[1]
    x = x_ref[0]
    h = _rms(x, g_ref[...]).astype(BF16)
    p = _dot(h, win_ref[...])
    u = p[:, :2 * ch] + bglu_ref[...]
    v = u[:, :ch] * _sigmoid(u[:, ch:])

    @pl.when(si == 0)
    def _():
        vfull[0:CONV_HALO, :] = jnp.zeros((CONV_HALO, ch), F32)

    vfull[CONV_HALO:CONV_HALO + t, :] = v
    for k in range(1, SUBLANES):
        vsh[k - 1, SUBLANES:CONV_HALO + t, :] = vfull[SUBLANES - k:CONV_HALO + t - k, :]

    def conv_rows(r, carry):
        base = r * CONV_ROWS
        for c in range(ch // LANES):
            cols = slice(c * LANES, (c + 1) * LANES)
            acc = jnp.zeros((CONV_ROWS // SUBLANES, SUBLANES, LANES), F32)
            for j in range(CONV_WIDTH):
                back = CONV_WIDTH - 1 - j
                k, m = back % SUBLANES, back // SUBLANES
                start = pl.multiple_of(base + (CONV_HALO - SUBLANES * m), SUBLANES)
                if k == 0:
                    blk = vfull[pl.ds(start, CONV_ROWS), cols]
                else:
                    blk = vsh[k - 1, pl.ds(start, CONV_ROWS), cols]
                acc = acc + blk.reshape(CONV_ROWS // SUBLANES, SUBLANES, LANES) * wdw_ref[j, :, cols][None]
            conv_s[pl.ds(pl.multiple_of(base, CONV_ROWS), CONV_ROWS), cols] = acc.reshape(CONV_ROWS, LANES)
        return carry

    lax.fori_loop(0, t // CONV_ROWS, conv_rows, 0)
    vfull[0:CONV_HALO, :] = vfull[t:t + CONV_HALO, :]

    cv = conv_s[...] + bdw_ref[...]
    mu = jnp.mean(cv, axis=-1, keepdims=True)
    cen = cv - mu
    var = jnp.mean(cen * cen, axis=-1, keepdims=True)
    y = cen * lax.rsqrt(var + LN_EPS) * lng_ref[...] + lnb_ref[...]
    mix = y * _sigmoid(y)

    mem_o = _mem_attention(p[:, 2 * ch:], mkv_ref[0, :, :MEM_WIDTH], mkv_ref[0, :, MEM_WIDTH:])
    cat = jnp.concatenate([mix.astype(BF16), mem_o.astype(BF16)], axis=-1)
    o_ref[0] = x + _dot(cat, wout_ref[...])


def _amix(x, layer, g, w_in, b_glu, w_dw8, b_dw, ln_g, ln_b, memkv, w_out, tile):
    b, s, d = x.shape
    ch = b_dw.shape[-1]
    mem_len = memkv.shape[1] // b
    tok = pl.BlockSpec((1, tile, d), lambda bi, si: (bi, si, 0))
    return pl.pallas_call(
        _amix_kernel,
        grid=(b, s // tile),
        in_specs=[
            tok,
            _const_spec(g.shape),
            _const_spec(w_in.shape),
            _const_spec(b_glu.shape),
            _const_spec(w_dw8.shape),
            _const_spec(b_dw.shape),
            _const_spec(ln_g.shape),
            _const_spec(ln_b.shape),
            pl.BlockSpec((1, mem_len, 2 * MEM_WIDTH), lambda bi, si: (layer, bi, 0)),
            _const_spec(w_out.shape),
        ],
        out_specs=tok,
        out_shape=jax.ShapeDtypeStruct(x.shape, F32),
        scratch_shapes=[
            pltpu.VMEM((CONV_HALO + tile, ch), F32),
            pltpu.VMEM((SUBLANES - 1, CONV_HALO + tile, ch), F32),
            pltpu.VMEM((tile, ch), F32),
        ],
        compiler_params=_params(),
        name=f"amix{layer}",
    )(x, g, w_in, b_glu, w_dw8, b_dw, ln_g, ln_b, memkv, w_out)


def _ffn_kernel(x_ref, g_ref, wup_ref, wdw_ref, bdw_ref, wdown_ref, gfin_ref, o_ref, act_s, ucarry,
                *, final_norm):
    si = pl.program_id(1)
    t = x_ref.shape[1]
    dff = wdown_ref.shape[0]
    x = x_ref[0]
    h = _rms(x, g_ref[...]).astype(BF16)

    @pl.when(si == 0)
    def _():
        ucarry[...] = jnp.zeros_like(ucarry)

    def conv3(cols):
        u = _dot(h, wup_ref[:, cols])
        ext = jnp.concatenate([ucarry[:, cols], u], axis=0)
        ucarry[:, cols] = u[t - SUBLANES:, :]
        w = wdw_ref[:, cols]
        return (u * w[2:3] + ext[SUBLANES - 1:SUBLANES - 1 + t] * w[1:2]
                + ext[SUBLANES - 2:SUBLANES - 2 + t] * w[0:1] + bdw_ref[:, cols])

    for c in range(dff // FFN_CHUNK):
        gate = conv3(slice(c * FFN_CHUNK, (c + 1) * FFN_CHUNK))
        val = conv3(slice(dff + c * FFN_CHUNK, dff + (c + 1) * FFN_CHUNK))
        act_s[:, c * FFN_CHUNK:(c + 1) * FFN_CHUNK] = (gate * _sigmoid(gate) * val).astype(BF16)

    y = x + _dot(act_s[...], wdown_ref[...])
    if final_norm:
        y = _rms(y, gfin_ref[...])
    o_ref[0] = y


def _ffn(x, g, w_up, w_dw, b_dw, w_down, g_final, final_norm, tile):
    b, s, d = x.shape
    dff = w_down.shape[0]
    assert dff % FFN_CHUNK == 0
    tok = pl.BlockSpec((1, tile, d), lambda bi, si: (bi, si, 0))
    return pl.pallas_call(
        functools.partial(_ffn_kernel, final_norm=final_norm),
        grid=(b, s // tile),
        in_specs=[
            tok,
            _const_spec(g.shape),
            _const_spec(w_up.shape),
            _const_spec(w_dw.shape),
            _const_spec(b_dw.shape),
            _const_spec(w_down.shape),
            _const_spec(g_final.shape),
        ],
        out_specs=tok,
        out_shape=jax.ShapeDtypeStruct(x.shape, F32),
        scratch_shapes=[
            pltpu.VMEM((tile, dff), BF16),
            pltpu.VMEM((SUBLANES, 2 * dff), F32),
        ],
        compiler_params=_params(),
        name="ffn_final" if final_norm else "ffn",
    )(x, g, w_up, w_dw, b_dw, w_down, g_final)


def _gate_selectors(n_heads, mixw):
    n_pairs = mixw // LANES
    key_sel = np.zeros((GATE_PARTS * LANES, n_pairs * LANES), np.float32)
    key_one = np.zeros((1, n_pairs * LANES), np.float32)
    qry_sel = np.zeros((GATE_PARTS * LANES, 2 * LANES), np.float32)
    qry_one = np.zeros((1, 2 * LANES), np.float32)
    for h in range(n_heads):
        kbase = (h // 2) * LANES + (h % 2) * BF16_ROWS
        qbase = h * BF16_ROWS
        for part in range(GATE_PARTS):
            key_one[0, kbase + part] = 1.0
            key_sel[part * LANES + h, kbase + GATE_PARTS + part] = -1.0
            qry_sel[part * LANES + h, qbase + part] = 1.0
            qry_one[0, qbase + GATE_PARTS + part] = 1.0
    return (jnp.asarray(key_sel, BF16), jnp.asarray(key_one), jnp.asarray(qry_sel, BF16), jnp.asarray(qry_one))


def _kvf_kernel(x_ref, g_ref, w_ref, bf_ref, ksel_ref, kone_ref, qsel_ref, qone_ref,
                ka_ref, vt_ref, qxt_ref, carry):
    si = pl.program_id(1)
    t = x_ref.shape[1]
    mixw = vt_ref.shape[2]
    tk = vt_ref.shape[3]

    @pl.when(si == 0)
    def _():
        carry[...] = jnp.zeros_like(carry)

    hk = _rms(x_ref[0], g_ref[...]).astype(BF16)
    kvf = _dot(hk, w_ref[...])
    f = kvf[:, 2 * mixw:] + bf_ref[...]
    log_f = jnp.minimum(f, 0.0) - jnp.log1p(jnp.exp(-jnp.abs(f)))
    tri = (lax.broadcasted_iota(jnp.int32, (t, t), 1) <= lax.broadcasted_iota(jnp.int32, (t, t), 0)).astype(F32)
    cum = jnp.dot(tri, log_f, precision=lax.Precision.HIGHEST, preferred_element_type=F32) + carry[0:1, :]
    carry[...] = jnp.broadcast_to(cum[t - 1:t, :], carry.shape)

    rest = cum * LOG2E
    pieces = []
    for _ in range(GATE_PARTS):
        piece = rest.astype(BF16)
        pieces.append(piece)
        rest = rest - piece.astype(F32)
    pieces = jnp.concatenate(pieces, axis=-1)
    kx = (_dot(pieces, ksel_ref[...]) + kone_ref[...]).astype(BF16)
    qx = _dot(pieces, qsel_ref[...]) + qone_ref[...]
    qxt_ref[0] = qx.T[:qxt_ref.shape[1], :].astype(BF16)

    k = kvf[:, :mixw].astype(BF16)
    for j in range(mixw // LANES):
        ka_ref[0, :, 2 * j * LANES:(2 * j + 1) * LANES] = k[:, j * LANES:(j + 1) * LANES]
        ka_ref[0, :, (2 * j + 1) * LANES:(2 * j + 2) * LANES] = kx[:, j * LANES:(j + 1) * LANES]
    vt = kvf[:, mixw:2 * mixw].T.astype(BF16)
    for r in range(t // tk):
        vt_ref[0, r] = vt[:, r * tk:(r + 1) * tk]


def _kvf(x, g, w_pad, bf_pad, selectors, mixw, n_heads, tile):
    b, s, d = x.shape
    tok = pl.BlockSpec((1, tile, d), lambda bi, si: (bi, si, 0))
    qx_rows = n_heads * BF16_ROWS
    return pl.pallas_call(
        _kvf_kernel,
        grid=(b, s // tile),
        in_specs=[tok, _const_spec(g.shape), _const_spec(w_pad.shape), _const_spec(bf_pad.shape)]
        + [_const_spec(a.shape) for a in selectors],
        out_specs=[
            pl.BlockSpec((1, tile, 2 * mixw), lambda bi, si: (bi, si, 0)),
            pl.BlockSpec((1, tile // FOX_TILE, mixw, FOX_TILE), lambda bi, si: (bi, si, 0, 0)),
            pl.BlockSpec((1, qx_rows, tile), lambda bi, si: (bi, 0, si)),
        ],
        out_shape=[
            jax.ShapeDtypeStruct((b, s, 2 * mixw), BF16),
            jax.ShapeDtypeStruct((b, s // FOX_TILE, mixw, FOX_TILE), BF16),
            jax.ShapeDtypeStruct((b, qx_rows, s), BF16),
        ],
        scratch_shapes=[pltpu.VMEM((SUBLANES, LANES), F32)],
        compiler_params=_params(),
        name="kvf",
    )(x, g, w_pad, bf_pad, *selectors)


def _qproj_kernel(x_ref, g_ref, w_ref, mkv_ref, qt_ref, mo_ref):
    mixw = qt_ref.shape[1]
    h = _rms(x_ref[0], g_ref[...]).astype(BF16)
    p = _dot(h, w_ref[...])
    qt_ref[0] = (p[:, :mixw] * (ATTN_SCALE * LOG2E)).T.astype(BF16)
    mo_ref[0] = _mem_attention(p[:, mixw:], mkv_ref[0, :, :MEM_WIDTH], mkv_ref[0, :, MEM_WIDTH:]).astype(BF16)


def _qproj(x, layer, g, w_in, memkv, mixw, tile):
    b, s, d = x.shape
    mem_len = memkv.shape[1] // b
    tok = pl.BlockSpec((1, tile, d), lambda bi, si: (bi, si, 0))
    return pl.pallas_call(
        _qproj_kernel,
        grid=(b, s // tile),
        in_specs=[
            tok,
            _const_spec(g.shape),
            _const_spec(w_in.shape),
            pl.BlockSpec((1, mem_len, 2 * MEM_WIDTH), lambda bi, si: (layer, bi, 0)),
        ],
        out_specs=[
            pl.BlockSpec((1, mixw, tile), lambda bi, si: (bi, 0, si)),
            pl.BlockSpec((1, tile, MEM_WIDTH), lambda bi, si: (bi, si, 0)),
        ],
        out_shape=[
            jax.ShapeDtypeStruct((b, mixw, s), BF16),
            jax.ShapeDtypeStruct((b, s, MEM_WIDTH), BF16),
        ],
        compiler_params=_params(),
        name=f"qproj{layer}",
    )(x, g, w_in, memkv)


def _fox_kernel(qt_ref, qxt_ref, ka_ref, vt_ref, o_ref, rhs_s, m_s, l_s, acc_s):
    qi = pl.program_id(1)
    tq = qt_ref.shape[2]
    tk = vt_ref.shape[3]
    n_heads = qt_ref.shape[1] // HEAD_DIM
    keep = lax.broadcasted_iota(jnp.int32, (tk, tq), 0) <= lax.broadcasted_iota(jnp.int32, (tk, tq), 1)

    def zeros(rows):
        return [jnp.zeros((rows, tq), BF16)] if rows else []

    for h in range(n_heads):
        hh = h % 2
        qh = qt_ref[0, h * HEAD_DIM:(h + 1) * HEAD_DIM, :]
        xh = qxt_ref[0, h * BF16_ROWS:(h + 1) * BF16_ROWS, :]
        rows = (zeros(hh * HEAD_DIM) + [qh] + zeros((1 - hh) * HEAD_DIM)
                + zeros(hh * BF16_ROWS) + [xh] + zeros(LANES - (hh + 1) * BF16_ROWS))
        rhs_s[h] = jnp.concatenate(rows, axis=0)
        m_s[h] = jnp.full((1, tq), NEG_BIG, F32)
        l_s[h] = jnp.zeros((1, tq), F32)
        acc_s[h] = jnp.zeros((HEAD_DIM, tq), F32)

    def step(i, diagonal):
        logits = []
        for h in range(n_heads):
            j = h // 2
            ka = ka_ref[0, pl.ds(pl.multiple_of(i * tk, tk), tk), 2 * j * LANES:(2 * j + 2) * LANES]
            logits.append(_dot(ka, rhs_s[h]))
        probs = []
        for h in range(n_heads):
            s = logits[h]
            if diagonal:
                s = jnp.where(keep, s, -jnp.inf)
            m = m_s[h]
            m_new = jnp.maximum(m, jnp.max(s, axis=0, keepdims=True))
            alpha = jnp.exp2(m - m_new)
            p = jnp.exp2(s - m_new)
            l_s[h] = alpha * l_s[h] + jnp.sum(p, axis=0, keepdims=True)
            m_s[h] = m_new
            probs.append((alpha, p.astype(BF16)))
        for h in range(n_heads):
            alpha, p = probs[h]
            vt = vt_ref[0, i, h * HEAD_DIM:(h + 1) * HEAD_DIM, :]
            acc_s[h] = alpha * acc_s[h] + _dot(vt, p)

    def full_step(i, carry):
        step(i, diagonal=False)
        return carry

    lax.fori_loop(0, qi, full_step, 0)
    step(qi, diagonal=True)
    for j in range(n_heads // 2):
        out_t = jnp.concatenate([acc_s[h] / l_s[h] for h in (2 * j, 2 * j + 1)], axis=0)
        o_ref[0, :, j * LANES:(j + 1) * LANES] = out_t.T.astype(BF16)


def _fox(qt, qxt, ka, vt):
    b, mixw, s = qt.shape
    tile = vt.shape[3]
    n_heads = mixw // HEAD_DIM
    return pl.pallas_call(
        _fox_kernel,
        grid=(b, s // tile),
        in_specs=[
            pl.BlockSpec((1, mixw, tile), lambda bi, si: (bi, 0, si)),
            pl.BlockSpec((1, qxt.shape[1], tile), lambda bi, si: (bi, 0, si)),
            pl.BlockSpec((1, s, ka.shape[2]), lambda bi, si: (bi, 0, 0)),
            pl.BlockSpec((1,) + vt.shape[1:], lambda bi, si: (bi, 0, 0, 0)),
        ],
        out_specs=pl.BlockSpec((1, tile, mixw), lambda bi, si: (bi, si, 0)),
        out_shape=jax.ShapeDtypeStruct((b, s, mixw), BF16),
        scratch_shapes=[
            pltpu.VMEM((n_heads, 2 * LANES, tile), BF16),
            pltpu.VMEM((n_heads, 1, tile), F32),
            pltpu.VMEM((n_heads, 1, tile), F32),
            pltpu.VMEM((n_heads, HEAD_DIM, tile), F32),
        ],
        compiler_params=_params(),
        name="fox",
    )(qt, qxt, ka, vt)


def _outproj_kernel(x_ref, mix_ref, mo_ref, w_ref, o_ref):
    cat = jnp.concatenate([mix_ref[0], mo_ref[0]], axis=-1)
    o_ref[0] = x_ref[0] + _dot(cat, w_ref[...])


def _outproj(x, mix, mo, w_out, tile):
    b, s, d = x.shape
    tok = pl.BlockSpec((1, tile, d), lambda bi, si: (bi, si, 0))
    return pl.pallas_call(
        _outproj_kernel,
        grid=(b, s // tile),
        in_specs=[
            tok,
            pl.BlockSpec((1, tile, mix.shape[2]), lambda bi, si: (bi, si, 0)),
            pl.BlockSpec((1, tile, mo.shape[2]), lambda bi, si: (bi, si, 0)),
            _const_spec(w_out.shape),
        ],
        out_specs=tok,
        out_shape=jax.ShapeDtypeStruct(x.shape, F32),
        compiler_params=_params(),
        name="outproj",
    )(x, mix, mo, w_out)


def kernel(x, mem, g_mix, w_in_a, b_glu, w_dw_a, b_dw_a, ln_g, ln_b, g_kv, w_kvf, b_f, w_in_b, g_mem,
           w_mem_kv, w_out, g_ffn, w_up, w_dw_f, b_dw_f, w_down, g_final):
    bsz, seq, d = x.shape
    depth = g_mix.shape[0]
    n_a = w_in_a.shape[0]
    mixw = w_dw_a.shape[2]
    n_heads = b_f.shape[0]
    assert n_heads * HEAD_DIM == mixw and mixw % LANES == 0 and n_heads <= LANES
    assert n_heads * BF16_ROWS <= 2 * LANES and 2 * BF16_ROWS <= LANES and 2 * GATE_PARTS <= BF16_ROWS

    row = lambda a: a.reshape(1, -1)
    memkv = _memkv(mem.reshape(-1, d), row(g_mem), w_mem_kv.astype(BF16))

    kvf_cols = 2 * mixw + LANES
    w_kvf_pad = jnp.pad(w_kvf, ((0, 0), (0, kvf_cols - w_kvf.shape[1]))).astype(BF16)
    bf_pad = jnp.pad(b_f, (0, LANES - n_heads)).reshape(1, LANES)
    selectors = _gate_selectors(n_heads, mixw)

    ka = vt = qxt = None
    for l in range(depth):
        if l < n_a:
            w_dw8 = jnp.broadcast_to(w_dw_a[l][:, None, :], (CONV_WIDTH, SUBLANES, mixw))
            x = _amix(x, l, row(g_mix[l]), w_in_a[l].astype(BF16), row(b_glu[l]), w_dw8, row(b_dw_a[l]),
                      row(ln_g[l]), row(ln_b[l]), memkv, w_out[l].astype(BF16), tile=256)
        else:
            if l == n_a:
                ka, vt, qxt = _kvf(x, row(g_kv), w_kvf_pad, bf_pad, selectors, mixw, n_heads, tile=512)
            qt, mem_o = _qproj(x, l, row(g_mix[l]), w_in_b[l - n_a].astype(BF16), memkv, mixw, tile=512)
            mix = _fox(qt, qxt, ka, vt)
            x = _outproj(x, mix, mem_o, w_out[l].astype(BF16), tile=512)
        x = _ffn(x, row(g_ffn[l]), w_up[l].astype(BF16), w_dw_f[l], row(b_dw_f[l]), w_down[l].astype(BF16),
                 row(g_final), final_norm=(l == depth - 1), tile=512)
    return x
```

```python
import functools
import math

import numpy as np
import jax
import jax.numpy as jnp
from jax import lax
from jax.experimental import pallas as pl
from jax.experimental.pallas import tpu as pltpu

F32 = jnp.float32
BF16 = jnp.bfloat16

HEAD_DIM = 64
N_MEM_HEADS = 4
MEM_WIDTH = N_MEM_HEADS * HEAD_DIM
CONV_WIDTH = 31
FFN_CONV_WIDTH = 3
RMS_EPS = 1e-6
LN_EPS = 1e-5
ATTN_SCALE = HEAD_DIM ** -0.5
LOG2E = math.log2(math.e)

LANES = 128
SUBLANES = 8
BF16_ROWS = 16
CONV_HALO = 32
CONV_ROWS = 128
FFN_CHUNK = 256
FOX_TILE = 256
GATE_PARTS = 3
GATE_GROUP = 16
NEG_BIG = -1e30

VMEM_LIMIT = 56 * 1024 * 1024


def _params(n_axes=2):
    return pltpu.CompilerParams(
        dimension_semantics=("arbitrary",) * n_axes, vmem_limit_bytes=VMEM_LIMIT)


def _const_spec(shape):
    nd = len(shape)
    return pl.BlockSpec(shape, lambda *_: (0,) * nd, pipeline_mode=pl.Buffered(1))


def _rms(x, g):
    return x * lax.rsqrt(jnp.mean(x * x, axis=-1, keepdims=True) + RMS_EPS) * g


def _sigmoid(x):
    return 1.0 / (1.0 + jnp.exp(-x))


def _dot(a, b):
    return jnp.dot(a, b, preferred_element_type=F32)


def _dot_nt(a, b):
    return lax.dot_general(a, b, (((1,), (1,)), ((), ())), preferred_element_type=F32)


def _mem_head_mask(h):
    lane = lax.broadcasted_iota(jnp.int32, (1, MEM_WIDTH), 1)
    return (lane >= h * HEAD_DIM) & (lane < (h + 1) * HEAD_DIM)


def _mem_logits(q_mem, mem_k):
    qs = q_mem * ATTN_SCALE
    return [_dot_nt(jnp.where(_mem_head_mask(h), qs, 0.0).astype(BF16), mem_k) for h in range(N_MEM_HEADS)]


def _mem_output(logits, mem_v):
    probs = []
    for s in logits:
        p = jnp.exp(s - jnp.max(s, axis=-1, keepdims=True))
        probs.append((p.astype(BF16), jnp.sum(p, axis=-1, keepdims=True)))
    out = None
    for h, (p, l) in enumerate(probs):
        o = _dot(p, mem_v) / l
        out = o if out is None else jnp.where(_mem_head_mask(h), o, out)
    return out


def _memkv_kernel(mem_ref, g_ref, w_ref, o_ref):
    mem_n = _rms(mem_ref[...], g_ref[...]).astype(BF16)
    o_ref[0] = _dot(mem_n, w_ref[0]).astype(BF16)


def _memkv(mem2d, g_mem, w_mem_kv):
    depth, d, n = w_mem_kv.shape
    rows = mem2d.shape[0]
    return pl.pallas_call(
        _memkv_kernel,
        grid=(depth,),
        in_specs=[
            _const_spec((rows, d)),
            _const_spec((1, d)),
            pl.BlockSpec((1, d, n), lambda l: (l, 0, 0)),
        ],
        out_specs=pl.BlockSpec((1, rows, n), lambda l: (l, 0, 0)),
        out_shape=jax.ShapeDtypeStruct((depth, rows, n), BF16),
        compiler_params=_params(1),
        name="memkv",
    )(mem2d, g_mem, w_mem_kv)


def _amix_kernel(x_ref, g_ref, win_ref, bglu_ref, wdw_ref, bdw_ref, lng_ref, lnb_ref, mkv_ref,
                 wout_ref, o_ref, vfull, conv_s):
    si = pl.program_id(1)
    t = x_ref.shape[1]
    n_slabs = vfull.shape[0]
    ch = n_slabs * LANES
    x = x_ref[0]
    h = _rms(x, g_ref[...]).astype(BF16)
    mem_logits = _mem_logits(_dot(h, win_ref[:, 2 * ch:]), mkv_ref[0, :, :MEM_WIDTH])
    u = _dot(h, win_ref[:, :2 * ch]) + bglu_ref[...]
    mem_o = _mem_output(mem_logits, mkv_ref[0, :, MEM_WIDTH:]).astype(BF16)
    v = u[:, :ch] * _sigmoid(u[:, ch:])

    @pl.when(si == 0)
    def _():
        vfull[:, 0:CONV_HALO, :] = jnp.zeros((n_slabs, CONV_HALO, LANES), F32)

    for c in range(n_slabs):
        vfull[c, CONV_HALO:CONV_HALO + t, :] = v[:, c * LANES:(c + 1) * LANES]

    def conv_rows(r, carry):
        base = r * CONV_ROWS
        for c in range(n_slabs):
            cols = slice(c * LANES, (c + 1) * LANES)
            acc = jnp.zeros((CONV_ROWS // SUBLANES, SUBLANES, LANES), F32)
            for j in range(CONV_WIDTH):
                back = CONV_WIDTH - 1 - j
                blk = vfull[c, pl.ds(base + (CONV_HALO - back), CONV_ROWS), :]
                acc = acc + blk.reshape(CONV_ROWS // SUBLANES, SUBLANES, LANES) * wdw_ref[j, :, cols][None]
            conv_s[pl.ds(pl.multiple_of(base, CONV_ROWS), CONV_ROWS), cols] = acc.reshape(CONV_ROWS, LANES)
        return carry

    lax.fori_loop(0, t // CONV_ROWS, conv_rows, 0)
    vfull[:, 0:CONV_HALO, :] = vfull[:, t:t + CONV_HALO, :]

    cv = conv_s[...] + bdw_ref[...]
    mu = jnp.mean(cv, axis=-1, keepdims=True)
    cen = cv - mu
    var = jnp.mean(cen * cen, axis=-1, keepdims=True)
    y = cen * lax.rsqrt(var + LN_EPS) * lng_ref[...] + lnb_ref[...]
    mix = y * _sigmoid(y)

    cat = jnp.concatenate([mix.astype(BF16), mem_o], axis=-1)
    o_ref[0] = x + _dot(cat, wout_ref[...])


def _amix(x, layer, g, w_in, b_glu, w_dw8, b_dw, ln_g, ln_b, memkv, w_out, tile):
    b, s, d = x.shape
    ch = b_dw.shape[-1]
    mem_len = memkv.shape[1] // b
    tok = pl.BlockSpec((1, tile, d), lambda bi, si: (bi, si, 0))
    return pl.pallas_call(
        _amix_kernel,
        grid=(b, s // tile),
        in_specs=[
            tok,
            _const_spec(g.shape),
            _const_spec(w_in.shape),
            _const_spec(b_glu.shape),
            _const_spec(w_dw8.shape),
            _const_spec(b_dw.shape),
            _const_spec(ln_g.shape),
            _const_spec(ln_b.shape),
            pl.BlockSpec((1, mem_len, 2 * MEM_WIDTH), lambda bi, si: (layer, bi, 0)),
            _const_spec(w_out.shape),
        ],
        out_specs=tok,
        out_shape=jax.ShapeDtypeStruct(x.shape, F32),
        scratch_shapes=[
            pltpu.VMEM((ch // LANES, CONV_HALO + tile, LANES), F32),
            pltpu.VMEM((tile, ch), F32),
        ],
        compiler_params=_params(),
        name=f"amix{layer}",
    )(x, g, w_in, b_glu, w_dw8, b_dw, ln_g, ln_b, memkv, w_out)


def _q_stage(x, g_ref, w_ref, mkv_ref, qt_ref, mo_ref):
    mixw = qt_ref.shape[1]
    h = _rms(x, g_ref[...]).astype(BF16)
    mem_logits = _mem_logits(_dot(h, w_ref[:, mixw:]), mkv_ref[0, :, :MEM_WIDTH])
    q = _dot(h, w_ref[:, :mixw])
    mo_ref[0] = _mem_output(mem_logits, mkv_ref[0, :, MEM_WIDTH:]).astype(BF16)
    qt_ref[0] = (q * (ATTN_SCALE * LOG2E)).T.astype(BF16)


def _ffn_kernel(*refs, pre_outproj, post):
    it = iter(refs)
    x_ref = next(it)
    if pre_outproj:
        mix_ref, moin_ref, wout_ref = next(it), next(it), next(it)
    g_ref, wup_ref, wdw_ref, bdw_ref, wdown_ref = (next(it) for _ in range(5))
    if post == "final":
        gfin_ref = next(it)
    elif post == "q":
        gq_ref, wq_ref, mkv_ref = next(it), next(it), next(it)
    o_ref = next(it)
    if post == "q":
        qt_ref, mo_ref = next(it), next(it)
    act_s, ucarry = next(it), next(it)

    si = pl.program_id(1)
    t = x_ref.shape[1]
    dff = wdown_ref.shape[0]
    x = x_ref[0]
    if pre_outproj:
        x = x + _dot(jnp.concatenate([mix_ref[0], moin_ref[0]], axis=-1), wout_ref[...])
    h = _rms(x, g_ref[...]).astype(BF16)

    @pl.when(si == 0)
    def _():
        ucarry[...] = jnp.zeros_like(ucarry)

    def conv3(cols):
        u = _dot(h, wup_ref[:, cols])
        ext = jnp.concatenate([ucarry[:, cols], u], axis=0)
        ucarry[:, cols] = u[t - SUBLANES:, :]
        w = wdw_ref[:, cols]
        return (u * w[2:3] + ext[SUBLANES - 1:SUBLANES - 1 + t] * w[1:2]
                + ext[SUBLANES - 2:SUBLANES - 2 + t] * w[0:1] + bdw_ref[:, cols])

    for c in range(dff // FFN_CHUNK):
        gate = conv3(slice(c * FFN_CHUNK, (c + 1) * FFN_CHUNK))
        val = conv3(slice(dff + c * FFN_CHUNK, dff + (c + 1) * FFN_CHUNK))
        act_s[:, c * FFN_CHUNK:(c + 1) * FFN_CHUNK] = (gate * _sigmoid(gate) * val).astype(BF16)

    y = x + _dot(act_s[...], wdown_ref[...])
    if post == "final":
        y = _rms(y, gfin_ref[...])
    o_ref[0] = y
    if post == "q":
        _q_stage(y, gq_ref, wq_ref, mkv_ref, qt_ref, mo_ref)


def _ffn(x, g, w_up, w_dw, b_dw, w_down, tile, outproj=None, g_final=None, q_next=None):
    b, s, d = x.shape
    dff = w_down.shape[0]
    assert dff % FFN_CHUNK == 0 and not (g_final is not None and q_next is not None)
    tok3 = lambda width: pl.BlockSpec((1, tile, width), lambda bi, si: (bi, si, 0))
    args, in_specs = [x], [tok3(d)]

    def add_const(*arrays):
        for a in arrays:
            args.append(a)
            in_specs.append(_const_spec(a.shape))

    if outproj is not None:
        mix, mem_o, w_out = outproj
        args += [mix, mem_o]
        in_specs += [tok3(mix.shape[2]), tok3(mem_o.shape[2])]
        add_const(w_out)
    add_const(g, w_up, w_dw, b_dw, w_down)
    out_specs, out_shape = [tok3(d)], [jax.ShapeDtypeStruct(x.shape, F32)]
    post = "none"
    if g_final is not None:
        post = "final"
        add_const(g_final)
    elif q_next is not None:
        post = "q"
        layer, g_q, w_q, memkv = q_next
        mixw = w_q.shape[1] - MEM_WIDTH
        add_const(g_q, w_q)
        args.append(memkv)
        in_specs.append(pl.BlockSpec((1, memkv.shape[1] // b, 2 * MEM_WIDTH), lambda bi, si: (layer, bi, 0)))
        out_specs += [pl.BlockSpec((1, mixw, tile), lambda bi, si: (bi, 0, si)), tok3(MEM_WIDTH)]
        out_shape += [jax.ShapeDtypeStruct((b, mixw, s), BF16), jax.ShapeDtypeStruct((b, s, MEM_WIDTH), BF16)]
    out = pl.pallas_call(
        functools.partial(_ffn_kernel, pre_outproj=outproj is not None, post=post),
        grid=(b, s // tile),
        in_specs=in_specs,
        out_specs=out_specs,
        out_shape=out_shape,
        scratch_shapes=[
            pltpu.VMEM((tile, dff), BF16),
            pltpu.VMEM((SUBLANES, 2 * dff), F32),
        ],
        compiler_params=_params(),
        name="ffn" + ("_o" if outproj is not None else "") + {"none": "", "final": "_final", "q": "_q"}[post],
    )(*args)
    return out[0] if post != "q" else out


def _spread_gate_columns(cols, n_heads):
    group = jnp.pad(cols, [(0, 0)] * (cols.ndim - 1) + [(0, GATE_GROUP - n_heads)])
    tiled = jnp.concatenate([group] * GATE_PARTS, axis=-1)
    return jnp.pad(tiled, [(0, 0)] * (cols.ndim - 1) + [(0, LANES - GATE_PARTS * GATE_GROUP)])


def _gate_selectors(n_heads, mixw):
    n_pairs = mixw // LANES
    key_sel = np.zeros((LANES, n_pairs * LANES), np.float32)
    key_one = np.zeros((1, n_pairs * LANES), np.float32)
    qry_sel = np.zeros((LANES, 2 * LANES), np.float32)
    qry_one = np.zeros((1, 2 * LANES), np.float32)
    for h in range(n_heads):
        kbase = (h // 2) * LANES + (h % 2) * BF16_ROWS
        qbase = h * BF16_ROWS
        for part in range(GATE_PARTS):
            key_one[0, kbase + part] = 1.0
            key_sel[part * GATE_GROUP + h, kbase + GATE_PARTS + part] = -1.0
            qry_sel[part * GATE_GROUP + h, qbase + part] = 1.0
            qry_one[0, qbase + GATE_PARTS + part] = 1.0
    return (jnp.asarray(key_sel, BF16), jnp.asarray(key_one), jnp.asarray(qry_sel, BF16), jnp.asarray(qry_one))


def _bf16_pieces(x):
    pieces = []
    for _ in range(GATE_PARTS):
        piece = x.astype(BF16)
        pieces.append(piece)
        x = x - piece.astype(F32)
    return pieces


def _kvq_kernel(x_ref, gkv_ref, wkvf_ref, bf_ref, ksel_ref, kone_ref, qsel_ref, qone_ref, gq_ref, wq_ref, mkv_ref,
                ka_ref, vt_ref, qxt_ref, qt_ref, mo_ref, carry):
    si = pl.program_id(1)
    t = x_ref.shape[1]
    mixw = vt_ref.shape[2]
    tk = vt_ref.shape[3]

    @pl.when(si == 0)
    def _():
        carry[...] = jnp.zeros_like(carry)

    x = x_ref[0]
    _q_stage(x, gq_ref, wq_ref, mkv_ref, qt_ref, mo_ref)

    hk = _rms(x, gkv_ref[...]).astype(BF16)
    kvf = _dot(hk, wkvf_ref[...])
    f = kvf[:, 2 * mixw:] + bf_ref[...]
    log_f = jnp.minimum(f, 0.0) - jnp.log1p(jnp.exp(-jnp.abs(f)))
    tri = jnp.where(lax.broadcasted_iota(jnp.int32, (t, t), 1) <= lax.broadcasted_iota(jnp.int32, (t, t), 0),
                    1.0, 0.0).astype(BF16)
    part_sums = _dot(tri, jnp.concatenate(_bf16_pieces(log_f), axis=-1))
    cum = carry[0:1, :] + sum(part_sums[:, p * LANES:(p + 1) * LANES] for p in range(GATE_PARTS))
    carry[...] = jnp.broadcast_to(cum[t - 1:t, :], carry.shape)

    lane = lax.broadcasted_iota(jnp.int32, (1, LANES), 1)
    pieces = _bf16_pieces(cum * LOG2E)
    packed = pieces[GATE_PARTS - 1]
    for p in range(GATE_PARTS - 2, -1, -1):
        packed = jnp.where(lane < (p + 1) * GATE_GROUP, pieces[p], packed)
    kx = (_dot(packed, ksel_ref[...]) + kone_ref[...]).astype(BF16)
    qx = _dot(packed, qsel_ref[...]) + qone_ref[...]
    qxt_ref[0] = qx.T[:qxt_ref.shape[1], :].astype(BF16)

    k = kvf[:, :mixw].astype(BF16)
    for j in range(mixw // LANES):
        ka_ref[0, :, 2 * j * LANES:(2 * j + 1) * LANES] = k[:, j * LANES:(j + 1) * LANES]
        ka_ref[0, :, (2 * j + 1) * LANES:(2 * j + 2) * LANES] = kx[:, j * LANES:(j + 1) * LANES]
    vt = kvf[:, mixw:2 * mixw].T.astype(BF16)
    for r in range(t // tk):
        vt_ref[0, r] = vt[:, r * tk:(r + 1) * tk]


def _kvq(x, g_kv, w_kvf_pad, bf_pad, selectors, layer, g_q, w_q, memkv, n_heads, tile):
    b, s, d = x.shape
    mixw = n_heads * HEAD_DIM
    qx_rows = n_heads * BF16_ROWS
    consts = (g_kv, w_kvf_pad, bf_pad) + tuple(selectors) + (g_q, w_q)
    return pl.pallas_call(
        _kvq_kernel,
        grid=(b, s // tile),
        in_specs=[pl.BlockSpec((1, tile, d), lambda bi, si: (bi, si, 0))]
        + [_const_spec(a.shape) for a in consts]
        + [pl.BlockSpec((1, memkv.shape[1] // b, 2 * MEM_WIDTH), lambda bi, si: (layer, bi, 0))],
        out_specs=[
            pl.BlockSpec((1, tile, 2 * mixw), lambda bi, si: (bi, si, 0)),
            pl.BlockSpec((1, tile // FOX_TILE, mixw, FOX_TILE), lambda bi, si: (bi, si, 0, 0)),
            pl.BlockSpec((1, qx_rows, tile), lambda bi, si: (bi, 0, si)),
            pl.BlockSpec((1, mixw, tile), lambda bi, si: (bi, 0, si)),
            pl.BlockSpec((1, tile, MEM_WIDTH), lambda bi, si: (bi, si, 0)),
        ],
        out_shape=[
            jax.ShapeDtypeStruct((b, s, 2 * mixw), BF16),
            jax.ShapeDtypeStruct((b, s // FOX_TILE, mixw, FOX_TILE), BF16),
            jax.ShapeDtypeStruct((b, qx_rows, s), BF16),
            jax.ShapeDtypeStruct((b, mixw, s), BF16),
            jax.ShapeDtypeStruct((b, s, MEM_WIDTH), BF16),
        ],
        scratch_shapes=[pltpu.VMEM((SUBLANES, LANES), F32)],
        compiler_params=_params(),
        name="kvq",
    )(x, *consts, memkv)


def _fox_kernel(qt_ref, qxt_ref, ka_ref, vt_ref, o_ref, rhs_s, m_s, l_s, acc_s, s_s):
    qi = pl.program_id(1)
    tq = qt_ref.shape[2]
    tk = vt_ref.shape[3]
    n_heads = qt_ref.shape[1] // HEAD_DIM
    keep = lax.broadcasted_iota(jnp.int32, (tk, tq), 0) <= lax.broadcasted_iota(jnp.int32, (tk, tq), 1)

    def zeros(rows):
        return [jnp.zeros((rows, tq), BF16)] if rows else []

    for h in range(n_heads):
        hh = h % 2
        qh = qt_ref[0, h * HEAD_DIM:(h + 1) * HEAD_DIM, :]
        xh = qxt_ref[0, h * BF16_ROWS:(h + 1) * BF16_ROWS, :]
        rows = (zeros(hh * HEAD_DIM) + [qh] + zeros((1 - hh) * HEAD_DIM)
                + zeros(hh * BF16_ROWS) + [xh] + zeros(LANES - (hh + 1) * BF16_ROWS))
        rhs_s[h] = jnp.concatenate(rows, axis=0)
        m_s[h] = jnp.full((1, tq), NEG_BIG, F32)
        l_s[h] = jnp.zeros((1, tq), F32)
        acc_s[h] = jnp.zeros((HEAD_DIM, tq), F32)

    def store_logits(i, h):
        j = h // 2
        ka = ka_ref[0, pl.ds(pl.multiple_of(i * tk, tk), tk), 2 * j * LANES:(2 * j + 2) * LANES]
        s_s[h] = _dot(ka, rhs_s[h])

    def softmax_pv(s, i, h, diagonal):
        if diagonal:
            s = jnp.where(keep, s, -jnp.inf)
        m = m_s[h]
        m_new = jnp.maximum(m, jnp.max(s, axis=0, keepdims=True))
        alpha = jnp.exp2(m - m_new)
        p = jnp.exp2(s - m_new)
        l_s[h] = alpha * l_s[h] + jnp.sum(p, axis=0, keepdims=True)
        m_s[h] = m_new
        vt = vt_ref[0, i, h * HEAD_DIM:(h + 1) * HEAD_DIM, :]
        acc_s[h] = alpha * acc_s[h] + _dot(vt, p.astype(BF16))

    for h in range(n_heads):
        store_logits(0, h)

    def pipelined(i, carry):
        for h in range(n_heads):
            s = s_s[h]
            store_logits(i + 1, h)
            softmax_pv(s, i, h, diagonal=False)
        return carry

    lax.fori_loop(0, qi, pipelined, 0)
    for h in range(n_heads):
        softmax_pv(s_s[h], qi, h, diagonal=True)
    for j in range(n_heads // 2):
        out_t = jnp.concatenate([acc_s[h] / l_s[h] for h in (2 * j, 2 * j + 1)], axis=0)
        o_ref[0, :, j * LANES:(j + 1) * LANES] = out_t.T.astype(BF16)


def _fox(qt, qxt, ka, vt):
    b, mixw, s = qt.shape
    tile = vt.shape[3]
    n_heads = mixw // HEAD_DIM
    return pl.pallas_call(
        _fox_kernel,
        grid=(b, s // tile),
        in_specs=[
            pl.BlockSpec((1, mixw, tile), lambda bi, si: (bi, 0, si)),
            pl.BlockSpec((1, qxt.shape[1], tile), lambda bi, si: (bi, 0, si)),
            pl.BlockSpec((1, s, ka.shape[2]), lambda bi, si: (bi, 0, 0)),
            pl.BlockSpec((1,) + vt.shape[1:], lambda bi, si: (bi, 0, 0, 0)),
        ],
        out_specs=pl.BlockSpec((1, tile, mixw), lambda bi, si: (bi, si, 0)),
        out_shape=jax.ShapeDtypeStruct((b, s, mixw), BF16),
        scratch_shapes=[
            pltpu.VMEM((n_heads, 2 * LANES, tile), BF16),
            pltpu.VMEM((n_heads, 1, tile), F32),
            pltpu.VMEM((n_heads, 1, tile), F32),
            pltpu.VMEM((n_heads, HEAD_DIM, tile), F32),
            pltpu.VMEM((n_heads, tile, tile), F32),
        ],
        compiler_params=_params(),
        name="fox",
    )(qt, qxt, ka, vt)


def kernel(x, mem, g_mix, w_in_a, b_glu, w_dw_a, b_dw_a, ln_g, ln_b, g_kv, w_kvf, b_f, w_in_b, g_mem,
           w_mem_kv, w_out, g_ffn, w_up, w_dw_f, b_dw_f, w_down, g_final):
    bsz, seq, d = x.shape
    depth = g_mix.shape[0]
    n_a = w_in_a.shape[0]
    mixw = w_dw_a.shape[2]
    n_heads = b_f.shape[0]
    assert 0 < n_a < depth
    assert n_heads * HEAD_DIM == mixw and mixw % LANES == 0 and n_heads <= GATE_GROUP
    assert GATE_PARTS * GATE_GROUP <= LANES and 2 * GATE_PARTS <= BF16_ROWS and n_heads * BF16_ROWS <= 2 * LANES

    row = lambda a: a.reshape(1, -1)
    bf16 = lambda a: a.astype(BF16)
    memkv = _memkv(mem.reshape(-1, d), row(g_mem), bf16(w_mem_kv))

    w_kvf_pad = bf16(jnp.concatenate([w_kvf[:, :2 * mixw], _spread_gate_columns(w_kvf[:, 2 * mixw:], n_heads)], axis=1))
    bf_pad = _spread_gate_columns(b_f.reshape(1, -1), n_heads)
    selectors = _gate_selectors(n_heads, mixw)

    ka = vt = qxt = qt = mem_o = None
    for l in range(depth):
        ffn_w = (row(g_ffn[l]), bf16(w_up[l]), w_dw_f[l], row(b_dw_f[l]), bf16(w_down[l]))
        if l < n_a:
            w_dw8 = jnp.broadcast_to(w_dw_a[l][:, None, :], (CONV_WIDTH, SUBLANES, mixw))
            x = _amix(x, l, row(g_mix[l]), bf16(w_in_a[l]), row(b_glu[l]), w_dw8, row(b_dw_a[l]),
                      row(ln_g[l]), row(ln_b[l]), memkv, bf16(w_out[l]), tile=256)
            x = _ffn(x, *ffn_w, tile=512)
            if l == n_a - 1:
                ka, vt, qxt, qt, mem_o = _kvq(x, row(g_kv), w_kvf_pad, bf_pad, selectors, n_a, row(g_mix[n_a]),
                                              bf16(w_in_b[0]), memkv, n_heads, tile=512)
        else:
            mix = _fox(qt, qxt, ka, vt)
            outproj = (mix, mem_o, bf16(w_out[l]))
            if l == depth - 1:
                x = _ffn(x, *ffn_w, tile=512, outproj=outproj, g_final=row(g_final))
            else:
                q_next = (l + 1, row(g_mix[l + 1]), bf16(w_in_b[l + 1 - n_a]), memkv)
                x, qt, mem_o = _ffn(x, *ffn_w, tile=512, outproj=outproj, q_next=q_next)
    return x
```

```python
import functools
import math

import numpy as np
import jax
import jax.numpy as jnp
from jax import lax
from jax.experimental import pallas as pl
from jax.experimental.pallas import tpu as pltpu

F32 = jnp.float32
BF16 = jnp.bfloat16

HEAD_DIM = 64
N_MEM_HEADS = 4
MEM_WIDTH = N_MEM_HEADS * HEAD_DIM
CONV_WIDTH = 31
FFN_CONV_WIDTH = 3
RMS_EPS = 1e-6
LN_EPS = 1e-5
ATTN_SCALE = HEAD_DIM ** -0.5
LOG2E = math.log2(math.e)

LANES = 128
SUBLANES = 8
BF16_ROWS = 16
CONV_HALO = 32
CONV_ROWS = 128
FFN_CHUNK = 256
FOX_TILE = 256
GATE_PARTS = 3
GATE_GROUP = 16
NEG_BIG = -1e30

VMEM_LIMIT = 56 * 1024 * 1024


def _params(n_axes=2):
    return pltpu.CompilerParams(
        dimension_semantics=("arbitrary",) * n_axes, vmem_limit_bytes=VMEM_LIMIT)


def _const_spec(shape):
    nd = len(shape)
    return pl.BlockSpec(shape, lambda *_: (0,) * nd, pipeline_mode=pl.Buffered(1))


def _token_call(kernel_fn, grid, args, in_specs, out_specs, out_shape, scratch_shapes, name, casts=()):
    n_in, n_out, n_cast = len(args), len(out_specs), len(casts)
    n_steps, n_s = grid[0] * grid[1], grid[1]
    args, in_specs, out_specs, out_shape = list(args), list(in_specs), list(out_specs), list(out_shape)
    for w, layer in casts:
        _, rows, cols = w.shape
        n_blocks = max(n for n in range(1, n_steps + 1) if rows % n == 0 and (rows // n) % BF16_ROWS == 0)
        block = lambda bi, si, n_blocks=n_blocks: jnp.minimum(bi * n_s + si, n_blocks - 1)
        args.append(w)
        in_specs.append(pl.BlockSpec((None, rows // n_blocks, cols),
                                     lambda bi, si, layer=layer, block=block: (layer, block(bi, si), 0)))
        out_specs.append(pl.BlockSpec((rows // n_blocks, cols), lambda bi, si, block=block: (block(bi, si), 0)))
        out_shape.append(jax.ShapeDtypeStruct((rows, cols), BF16))

    def body(*refs):
        cast_in = refs[n_in:n_in + n_cast]
        outs = refs[n_in + n_cast:n_in + n_cast + n_out]
        cast_out = refs[n_in + n_cast + n_out:n_in + 2 * n_cast + n_out]
        for src, dst in zip(cast_in, cast_out):
            dst[...] = src[...].astype(BF16)
        kernel_fn(*refs[:n_in], *outs, *refs[n_in + 2 * n_cast + n_out:])

    res = pl.pallas_call(body, grid=grid, in_specs=in_specs, out_specs=out_specs, out_shape=out_shape,
                         scratch_shapes=scratch_shapes, compiler_params=_params(), name=name)(*args)
    return list(res[:n_out]), list(res[n_out:])


def _rms(x, g):
    return x * lax.rsqrt(jnp.mean(x * x, axis=-1, keepdims=True) + RMS_EPS) * g


def _sigmoid(x):
    return 1.0 / (1.0 + jnp.exp(-x))


def _dot(a, b):
    return jnp.dot(a, b, preferred_element_type=F32)


def _dot_nt(a, b):
    return lax.dot_general(a, b, (((1,), (1,)), ((), ())), preferred_element_type=F32)


def _mem_head_mask(h):
    lane = lax.broadcasted_iota(jnp.int32, (1, MEM_WIDTH), 1)
    return (lane >= h * HEAD_DIM) & (lane < (h + 1) * HEAD_DIM)


def _mem_logits(q_mem, mem_k):
    qs = q_mem * ATTN_SCALE
    return [_dot_nt(jnp.where(_mem_head_mask(h), qs, 0.0).astype(BF16), mem_k) for h in range(N_MEM_HEADS)]


def _mem_output(logits, mem_v):
    probs = []
    for s in logits:
        p = jnp.exp(s - jnp.max(s, axis=-1, keepdims=True))
        probs.append((p.astype(BF16), jnp.sum(p, axis=-1, keepdims=True)))
    out = None
    for h, (p, l) in enumerate(probs):
        o = _dot(p, mem_v) / l
        out = o if out is None else jnp.where(_mem_head_mask(h), o, out)
    return out


def _memkv_kernel(mem_ref, g_ref, w_ref, o_ref):
    mem_n = _rms(mem_ref[...], g_ref[...]).astype(BF16)
    o_ref[0] = _dot(mem_n, w_ref[0]).astype(BF16)


def _memkv(mem2d, g_mem, w_mem_kv):
    depth, d, n = w_mem_kv.shape
    rows = mem2d.shape[0]
    return pl.pallas_call(
        _memkv_kernel,
        grid=(depth,),
        in_specs=[
            _const_spec((rows, d)),
            _const_spec((1, d)),
            pl.BlockSpec((1, d, n), lambda l: (l, 0, 0)),
        ],
        out_specs=pl.BlockSpec((1, rows, n), lambda l: (l, 0, 0)),
        out_shape=jax.ShapeDtypeStruct((depth, rows, n), BF16),
        compiler_params=_params(1),
        name="memkv",
    )(mem2d, g_mem, w_mem_kv)


def _amix_kernel(x_ref, g_ref, win_ref, bglu_ref, wdw_ref, bdw_ref, lng_ref, lnb_ref, mkv_ref,
                 wout_ref, o_ref, vfull, conv_s):
    si = pl.program_id(1)
    t = x_ref.shape[1]
    n_slabs = vfull.shape[0]
    ch = n_slabs * LANES
    x = x_ref[0]
    h = _rms(x, g_ref[...]).astype(BF16)
    mem_logits = _mem_logits(_dot(h, win_ref[:, 2 * ch:]), mkv_ref[0, :, :MEM_WIDTH])
    u = _dot(h, win_ref[:, :2 * ch]) + bglu_ref[...]
    mem_o = _mem_output(mem_logits, mkv_ref[0, :, MEM_WIDTH:]).astype(BF16)
    v = u[:, :ch] * _sigmoid(u[:, ch:])

    @pl.when(si == 0)
    def _():
        vfull[:, 0:CONV_HALO, :] = jnp.zeros((n_slabs, CONV_HALO, LANES), F32)

    for c in range(n_slabs):
        vfull[c, CONV_HALO:CONV_HALO + t, :] = v[:, c * LANES:(c + 1) * LANES]

    def conv_rows(r, carry):
        base = r * CONV_ROWS
        for c in range(n_slabs):
            cols = slice(c * LANES, (c + 1) * LANES)
            acc = jnp.zeros((CONV_ROWS // SUBLANES, SUBLANES, LANES), F32)
            for j in range(CONV_WIDTH):
                back = CONV_WIDTH - 1 - j
                blk = vfull[c, pl.ds(base + (CONV_HALO - back), CONV_ROWS), :]
                acc = acc + blk.reshape(CONV_ROWS // SUBLANES, SUBLANES, LANES) * wdw_ref[j, :, cols][None]
            conv_s[pl.ds(pl.multiple_of(base, CONV_ROWS), CONV_ROWS), cols] = acc.reshape(CONV_ROWS, LANES)
        return carry

    lax.fori_loop(0, t // CONV_ROWS, conv_rows, 0)
    vfull[:, 0:CONV_HALO, :] = vfull[:, t:t + CONV_HALO, :]

    cv = conv_s[...] + bdw_ref[...]
    mu = jnp.mean(cv, axis=-1, keepdims=True)
    cen = cv - mu
    var = jnp.mean(cen * cen, axis=-1, keepdims=True)
    y = cen * lax.rsqrt(var + LN_EPS) * lng_ref[...] + lnb_ref[...]
    mix = y * _sigmoid(y)

    cat = jnp.concatenate([mix.astype(BF16), mem_o], axis=-1)
    o_ref[0] = x + _dot(cat, wout_ref[...])


def _amix(x, layer, g, w_in, b_glu, w_dw8, b_dw, ln_g, ln_b, memkv, w_out, tile, casts=()):
    b, s, d = x.shape
    ch = b_dw.shape[-1]
    mem_len = memkv.shape[1] // b
    tok = pl.BlockSpec((1, tile, d), lambda bi, si: (bi, si, 0))
    (y,), cast = _token_call(
        _amix_kernel,
        grid=(b, s // tile),
        args=(x, g, w_in, b_glu, w_dw8, b_dw, ln_g, ln_b, memkv, w_out),
        in_specs=[
            tok,
            _const_spec(g.shape),
            _const_spec(w_in.shape),
            _const_spec(b_glu.shape),
            _const_spec(w_dw8.shape),
            _const_spec(b_dw.shape),
            _const_spec(ln_g.shape),
            _const_spec(ln_b.shape),
            pl.BlockSpec((1, mem_len, 2 * MEM_WIDTH), lambda bi, si: (layer, bi, 0)),
            _const_spec(w_out.shape),
        ],
        out_specs=[tok],
        out_shape=[jax.ShapeDtypeStruct(x.shape, F32)],
        scratch_shapes=[
            pltpu.VMEM((ch // LANES, CONV_HALO + tile, LANES), F32),
            pltpu.VMEM((tile, ch), F32),
        ],
        name=f"amix{layer}",
        casts=casts,
    )
    return y, cast


def _q_stage(x, g_ref, w_ref, mkv_ref, qt_ref, mo_ref):
    mixw = qt_ref.shape[1]
    h = _rms(x, g_ref[...]).astype(BF16)
    mem_logits = _mem_logits(_dot(h, w_ref[:, mixw:]), mkv_ref[0, :, :MEM_WIDTH])
    q = _dot(h, w_ref[:, :mixw])
    mo_ref[0] = _mem_output(mem_logits, mkv_ref[0, :, MEM_WIDTH:]).astype(BF16)
    qt_ref[0] = (q * (ATTN_SCALE * LOG2E)).T.astype(BF16)


def _ffn_kernel(*refs, pre_outproj, post):
    it = iter(refs)
    x_ref = next(it)
    if pre_outproj:
        mix_ref, moin_ref, wout_ref = next(it), next(it), next(it)
    g_ref, wup_ref, wdw_ref, bdw_ref, wdown_ref = (next(it) for _ in range(5))
    if post == "final":
        gfin_ref = next(it)
    elif post == "q":
        gq_ref, wq_ref, mkv_ref = next(it), next(it), next(it)
    o_ref = next(it)
    if post == "q":
        qt_ref, mo_ref = next(it), next(it)
    act_s, ucarry = next(it), next(it)

    si = pl.program_id(1)
    t = x_ref.shape[1]
    dff = wdown_ref.shape[0]
    x = x_ref[0]
    if pre_outproj:
        x = x + _dot(jnp.concatenate([mix_ref[0], moin_ref[0]], axis=-1), wout_ref[...])
    h = _rms(x, g_ref[...]).astype(BF16)

    @pl.when(si == 0)
    def _():
        ucarry[...] = jnp.zeros_like(ucarry)

    def conv3(cols):
        u = _dot(h, wup_ref[:, cols])
        ext = jnp.concatenate([ucarry[:, cols], u], axis=0)
        ucarry[:, cols] = u[t - SUBLANES:, :]
        w = wdw_ref[:, cols]
        return (u * w[2:3] + ext[SUBLANES - 1:SUBLANES - 1 + t] * w[1:2]
                + ext[SUBLANES - 2:SUBLANES - 2 + t] * w[0:1] + bdw_ref[:, cols])

    for c in range(dff // FFN_CHUNK):
        gate = conv3(slice(c * FFN_CHUNK, (c + 1) * FFN_CHUNK))
        val = conv3(slice(dff + c * FFN_CHUNK, dff + (c + 1) * FFN_CHUNK))
        act_s[:, c * FFN_CHUNK:(c + 1) * FFN_CHUNK] = (gate * _sigmoid(gate) * val).astype(BF16)

    y = x + _dot(act_s[...], wdown_ref[...])
    if post == "final":
        y = _rms(y, gfin_ref[...])
    o_ref[0] = y
    if post == "q":
        _q_stage(y, gq_ref, wq_ref, mkv_ref, qt_ref, mo_ref)


def _ffn(x, g, w_up, w_dw, b_dw, w_down, tile, outproj=None, g_final=None, q_next=None, casts=()):
    b, s, d = x.shape
    dff = w_down.shape[0]
    assert dff % FFN_CHUNK == 0 and not (g_final is not None and q_next is not None)
    tok3 = lambda width: pl.BlockSpec((1, tile, width), lambda bi, si: (bi, si, 0))
    args, in_specs = [x], [tok3(d)]

    def add_const(*arrays):
        for a in arrays:
            args.append(a)
            in_specs.append(_const_spec(a.shape))

    if outproj is not None:
        mix, mem_o, w_out = outproj
        args += [mix, mem_o]
        in_specs += [tok3(mix.shape[2]), tok3(mem_o.shape[2])]
        add_const(w_out)
    add_const(g, w_up, w_dw, b_dw, w_down)
    out_specs, out_shape = [tok3(d)], [jax.ShapeDtypeStruct(x.shape, F32)]
    post = "none"
    if g_final is not None:
        post = "final"
        add_const(g_final)
    elif q_next is not None:
        post = "q"
        layer, g_q, w_q, memkv = q_next
        mixw = w_q.shape[1] - MEM_WIDTH
        add_const(g_q, w_q)
        args.append(memkv)
        in_specs.append(pl.BlockSpec((1, memkv.shape[1] // b, 2 * MEM_WIDTH), lambda bi, si: (layer, bi, 0)))
        out_specs += [pl.BlockSpec((1, mixw, tile), lambda bi, si: (bi, 0, si)), tok3(MEM_WIDTH)]
        out_shape += [jax.ShapeDtypeStruct((b, mixw, s), BF16), jax.ShapeDtypeStruct((b, s, MEM_WIDTH), BF16)]
    out, cast = _token_call(
        functools.partial(_ffn_kernel, pre_outproj=outproj is not None, post=post),
        grid=(b, s // tile),
        args=args,
        in_specs=in_specs,
        out_specs=out_specs,
        out_shape=out_shape,
        scratch_shapes=[
            pltpu.VMEM((tile, dff), BF16),
            pltpu.VMEM((SUBLANES, 2 * dff), F32),
        ],
        name="ffn" + ("_o" if outproj is not None else "") + {"none": "", "final": "_final", "q": "_q"}[post],
        casts=casts,
    )
    return (out[0] if post != "q" else out), cast


def _spread_gate_columns(cols, n_heads):
    group = jnp.pad(cols, [(0, 0)] * (cols.ndim - 1) + [(0, GATE_GROUP - n_heads)])
    tiled = jnp.concatenate([group] * GATE_PARTS, axis=-1)
    return jnp.pad(tiled, [(0, 0)] * (cols.ndim - 1) + [(0, LANES - GATE_PARTS * GATE_GROUP)])


def _gate_selectors(n_heads, mixw):
    n_pairs = mixw // LANES
    key_sel = np.zeros((LANES, n_pairs * LANES), np.float32)
    key_one = np.zeros((1, n_pairs * LANES), np.float32)
    qry_sel = np.zeros((LANES, 2 * LANES), np.float32)
    qry_one = np.zeros((1, 2 * LANES), np.float32)
    for h in range(n_heads):
        kbase = (h // 2) * LANES + (h % 2) * BF16_ROWS
        qbase = h * BF16_ROWS
        for part in range(GATE_PARTS):
            key_one[0, kbase + part] = 1.0
            key_sel[part * GATE_GROUP + h, kbase + GATE_PARTS + part] = -1.0
            qry_sel[part * GATE_GROUP + h, qbase + part] = 1.0
            qry_one[0, qbase + GATE_PARTS + part] = 1.0
    return (jnp.asarray(key_sel, BF16), jnp.asarray(key_one), jnp.asarray(qry_sel, BF16), jnp.asarray(qry_one))


def _bf16_pieces(x):
    pieces = []
    for _ in range(GATE_PARTS):
        piece = x.astype(BF16)
        pieces.append(piece)
        x = x - piece.astype(F32)
    return pieces


def _kvq_kernel(x_ref, gkv_ref, wkvf_ref, bf_ref, ksel_ref, kone_ref, qsel_ref, qone_ref, gq_ref, wq_ref, mkv_ref,
                ka_ref, vt_ref, qxt_ref, qt_ref, mo_ref, carry):
    si = pl.program_id(1)
    t = x_ref.shape[1]
    mixw = vt_ref.shape[2]
    tk = vt_ref.shape[3]

    @pl.when(si == 0)
    def _():
        carry[...] = jnp.zeros_like(carry)

    x = x_ref[0]
    _q_stage(x, gq_ref, wq_ref, mkv_ref, qt_ref, mo_ref)

    hk = _rms(x, gkv_ref[...]).astype(BF16)
    kvf = _dot(hk, wkvf_ref[...])
    f = kvf[:, 2 * mixw:] + bf_ref[...]
    log_f = jnp.minimum(f, 0.0) - jnp.log1p(jnp.exp(-jnp.abs(f)))
    tri = jnp.where(lax.broadcasted_iota(jnp.int32, (t, t), 1) <= lax.broadcasted_iota(jnp.int32, (t, t), 0),
                    1.0, 0.0).astype(BF16)
    part_sums = _dot(tri, jnp.concatenate(_bf16_pieces(log_f), axis=-1))
    cum = carry[0:1, :] + sum(part_sums[:, p * LANES:(p + 1) * LANES] for p in range(GATE_PARTS))
    carry[...] = jnp.broadcast_to(cum[t - 1:t, :], carry.shape)

    lane = lax.broadcasted_iota(jnp.int32, (1, LANES), 1)
    pieces = _bf16_pieces(cum * LOG2E)
    packed = pieces[GATE_PARTS - 1]
    for p in range(GATE_PARTS - 2, -1, -1):
        packed = jnp.where(lane < (p + 1) * GATE_GROUP, pieces[p], packed)
    kx = (_dot(packed, ksel_ref[...]) + kone_ref[...]).astype(BF16)
    qx = _dot(packed, qsel_ref[...]) + qone_ref[...]
    qxt_ref[0] = qx.T[:qxt_ref.shape[1], :].astype(BF16)

    k = kvf[:, :mixw].astype(BF16)
    for j in range(mixw // LANES):
        ka_ref[0, :, 2 * j * LANES:(2 * j + 1) * LANES] = k[:, j * LANES:(j + 1) * LANES]
        ka_ref[0, :, (2 * j + 1) * LANES:(2 * j + 2) * LANES] = kx[:, j * LANES:(j + 1) * LANES]
    vt = kvf[:, mixw:2 * mixw].T.astype(BF16)
    for r in range(t // tk):
        vt_ref[0, r] = vt[:, r * tk:(r + 1) * tk]


def _kvq(x, g_kv, w_kvf_pad, bf_pad, selectors, layer, g_q, w_q, memkv, n_heads, tile):
    b, s, d = x.shape
    mixw = n_heads * HEAD_DIM
    qx_rows = n_heads * BF16_ROWS
    consts = (g_kv, w_kvf_pad, bf_pad) + tuple(selectors) + (g_q, w_q)
    return pl.pallas_call(
        _kvq_kernel,
        grid=(b, s // tile),
        in_specs=[pl.BlockSpec((1, tile, d), lambda bi, si: (bi, si, 0))]
        + [_const_spec(a.shape) for a in consts]
        + [pl.BlockSpec((1, memkv.shape[1] // b, 2 * MEM_WIDTH), lambda bi, si: (layer, bi, 0))],
        out_specs=[
            pl.BlockSpec((1, tile, 2 * mixw), lambda bi, si: (bi, si, 0)),
            pl.BlockSpec((1, tile // FOX_TILE, mixw, FOX_TILE), lambda bi, si: (bi, si, 0, 0)),
            pl.BlockSpec((1, qx_rows, tile), lambda bi, si: (bi, 0, si)),
            pl.BlockSpec((1, mixw, tile), lambda bi, si: (bi, 0, si)),
            pl.BlockSpec((1, tile, MEM_WIDTH), lambda bi, si: (bi, si, 0)),
        ],
        out_shape=[
            jax.ShapeDtypeStruct((b, s, 2 * mixw), BF16),
            jax.ShapeDtypeStruct((b, s // FOX_TILE, mixw, FOX_TILE), BF16),
            jax.ShapeDtypeStruct((b, qx_rows, s), BF16),
            jax.ShapeDtypeStruct((b, mixw, s), BF16),
            jax.ShapeDtypeStruct((b, s, MEM_WIDTH), BF16),
        ],
        scratch_shapes=[pltpu.VMEM((SUBLANES, LANES), F32)],
        compiler_params=_params(),
        name="kvq",
    )(x, *consts, memkv)


def _fox_kernel(qt_ref, qxt_ref, ka_ref, vt_ref, o_ref, rhs_s, m_s, l_s, acc_s, s_s):
    qi = pl.program_id(1)
    tq = qt_ref.shape[2]
    tk = vt_ref.shape[3]
    n_heads = qt_ref.shape[1] // HEAD_DIM
    keep = lax.broadcasted_iota(jnp.int32, (tk, tq), 0) <= lax.broadcasted_iota(jnp.int32, (tk, tq), 1)

    def zeros(rows):
        return [jnp.zeros((rows, tq), BF16)] if rows else []

    for h in range(n_heads):
        hh = h % 2
        qh = qt_ref[0, h * HEAD_DIM:(h + 1) * HEAD_DIM, :]
        xh = qxt_ref[0, h * BF16_ROWS:(h + 1) * BF16_ROWS, :]
        rows = (zeros(hh * HEAD_DIM) + [qh] + zeros((1 - hh) * HEAD_DIM)
                + zeros(hh * BF16_ROWS) + [xh] + zeros(LANES - (hh + 1) * BF16_ROWS))
        rhs_s[h] = jnp.concatenate(rows, axis=0)
        m_s[h] = jnp.full((1, tq), NEG_BIG, F32)
        l_s[h] = jnp.zeros((1, tq), F32)
        acc_s[h] = jnp.zeros((HEAD_DIM, tq), F32)

    def store_logits(i, h):
        j = h // 2
        ka = ka_ref[0, pl.ds(pl.multiple_of(i * tk, tk), tk), 2 * j * LANES:(2 * j + 2) * LANES]
        s_s[h] = _dot(ka, rhs_s[h])

    def softmax_pv(s, i, h, diagonal):
        if diagonal:
            s = jnp.where(keep, s, -jnp.inf)
        m = m_s[h]
        m_new = jnp.maximum(m, jnp.max(s, axis=0, keepdims=True))
        alpha = jnp.exp2(m - m_new)
        p = jnp.exp2(s - m_new)
        l_s[h] = alpha * l_s[h] + jnp.sum(p, axis=0, keepdims=True)
        m_s[h] = m_new
        vt = vt_ref[0, i, h * HEAD_DIM:(h + 1) * HEAD_DIM, :]
        acc_s[h] = alpha * acc_s[h] + _dot(vt, p.astype(BF16))

    for h in range(n_heads):
        store_logits(0, h)

    def pipelined(i, carry):
        for h in range(n_heads):
            s = s_s[h]
            store_logits(i + 1, h)
            softmax_pv(s, i, h, diagonal=False)
        return carry

    lax.fori_loop(0, qi, pipelined, 0)
    for h in range(n_heads):
        softmax_pv(s_s[h], qi, h, diagonal=True)
    for j in range(n_heads // 2):
        out_t = jnp.concatenate([acc_s[h] / l_s[h] for h in (2 * j, 2 * j + 1)], axis=0)
        o_ref[0, :, j * LANES:(j + 1) * LANES] = out_t.T.astype(BF16)


def _fox(qt, qxt, ka, vt, casts=()):
    b, mixw, s = qt.shape
    tile = vt.shape[3]
    n_heads = mixw // HEAD_DIM
    (mix,), cast = _token_call(
        _fox_kernel,
        grid=(b, s // tile),
        args=(qt, qxt, ka, vt),
        in_specs=[
            pl.BlockSpec((1, mixw, tile), lambda bi, si: (bi, 0, si)),
            pl.BlockSpec((1, qxt.shape[1], tile), lambda bi, si: (bi, 0, si)),
            pl.BlockSpec((1, s, ka.shape[2]), lambda bi, si: (bi, 0, 0)),
            pl.BlockSpec((1,) + vt.shape[1:], lambda bi, si: (bi, 0, 0, 0)),
        ],
        out_specs=[pl.BlockSpec((1, tile, mixw), lambda bi, si: (bi, si, 0))],
        out_shape=[jax.ShapeDtypeStruct((b, s, mixw), BF16)],
        scratch_shapes=[
            pltpu.VMEM((n_heads, 2 * LANES, tile), BF16),
            pltpu.VMEM((n_heads, 1, tile), F32),
            pltpu.VMEM((n_heads, 1, tile), F32),
            pltpu.VMEM((n_heads, HEAD_DIM, tile), F32),
            pltpu.VMEM((n_heads, tile, tile), F32),
        ],
        name="fox",
        casts=casts,
    )
    return mix, cast


def kernel(x, mem, g_mix, w_in_a, b_glu, w_dw_a, b_dw_a, ln_g, ln_b, g_kv, w_kvf, b_f, w_in_b, g_mem,
           w_mem_kv, w_out, g_ffn, w_up, w_dw_f, b_dw_f, w_down, g_final):
    bsz, seq, d = x.shape
    depth = g_mix.shape[0]
    n_a = w_in_a.shape[0]
    mixw = w_dw_a.shape[2]
    n_heads = b_f.shape[0]
    assert 0 < n_a < depth
    assert n_heads * HEAD_DIM == mixw and mixw % LANES == 0 and n_heads <= GATE_GROUP
    assert GATE_PARTS * GATE_GROUP <= LANES and 2 * GATE_PARTS <= BF16_ROWS and n_heads * BF16_ROWS <= 2 * LANES

    row = lambda a: a.reshape(1, -1)
    bf16 = lambda a: a.astype(BF16)
    memkv = _memkv(mem.reshape(-1, d), row(g_mem), bf16(w_mem_kv))

    w_kvf_pad = bf16(jnp.concatenate([w_kvf[:, :2 * mixw], _spread_gate_columns(w_kvf[:, 2 * mixw:], n_heads)], axis=1))
    bf_pad = _spread_gate_columns(b_f.reshape(1, -1), n_heads)
    selectors = _gate_selectors(n_heads, mixw)

    ka = vt = qxt = qt = mem_o = None
    mixer_w = [bf16(w_in_a[0]), bf16(w_out[0])]
    for l in range(depth):
        ffn_small = (w_dw_f[l], row(b_dw_f[l]))
        if l < n_a:
            w_dw8 = jnp.broadcast_to(w_dw_a[l][:, None, :], (CONV_WIDTH, SUBLANES, mixw))
            x, (w_up_l, w_down_l) = _amix(
                x, l, row(g_mix[l]), mixer_w[0], row(b_glu[l]), w_dw8, row(b_dw_a[l]), row(ln_g[l]), row(ln_b[l]),
                memkv, mixer_w[1], tile=256, casts=[(w_up, l), (w_down, l)])
            next_mixer = [(w_in_a, l + 1), (w_out, l + 1)] if l + 1 < n_a else [(w_in_b, 0)]
            x, mixer_w = _ffn(x, row(g_ffn[l]), w_up_l, *ffn_small, w_down_l, tile=512, casts=next_mixer)
            if l == n_a - 1:
                ka, vt, qxt, qt, mem_o = _kvq(x, row(g_kv), w_kvf_pad, bf_pad, selectors, n_a, row(g_mix[n_a]),
                                              mixer_w[0], memkv, n_heads, tile=512)
        else:
            last = l == depth - 1
            casts = [(w_out, l), (w_up, l), (w_down, l)] + ([] if last else [(w_in_b, l + 1 - n_a)])
            mix, cast = _fox(qt, qxt, ka, vt, casts=casts)
            ffn_args = (x, row(g_ffn[l]), cast[1], *ffn_small, cast[2])
            outproj = (mix, mem_o, cast[0])
            if last:
                x, _ = _ffn(*ffn_args, tile=512, outproj=outproj, g_final=row(g_final))
            else:
                (x, qt, mem_o), _ = _ffn(*ffn_args, tile=512, outproj=outproj,
                                         q_next=(l + 1, row(g_mix[l + 1]), cast[3], memkv))
    return x
```
